```python
import math
import jax
import jax.numpy as jnp
from jax import lax
import numpy as np


D_MODEL = 4096
BATCH = 1
SEQ = 16384
DEPTH = 2
DEC_BATCH = 1
DEC_SEQ = 8192
PAST_LEN = 128

GRID_W = 64
BRANCH_W = 3 * D_MODEL // 8
NA_HEAD_DIM = 128
NA_HEADS = BRANCH_W // NA_HEAD_DIM
NA_KH = 8
NA_KW = 16
S5_GROUP = 16
S5_GROUPS = BRANCH_W // S5_GROUP
S5_STATE = 64
DIFF_HEAD_DIM = 64
DIFF_V_DIM = 2 * DIFF_HEAD_DIM
DIFF_HEADS = BRANCH_W // DIFF_V_DIM
D_FF = 11008
N_SUB = 3
N_BRANCH = 3
IN_COLS = 3 * BRANCH_W + BRANCH_W + 4 * DIFF_HEADS * DIFF_HEAD_DIM + BRANCH_W + N_BRANCH * D_MODEL
ROPE_THETA = 10000.0
RMS_EPS = 1e-6
Q_BLOCK = 128
NEG_INF = -1e30

kernel_name = "hybrid_na_s5_diffattn_encoder"


def rmsnorm(x, g):
    x32 = x.astype(jnp.float32)
    y = x32 * lax.rsqrt(jnp.mean(x32 * x32, axis=-1, keepdims=True) + RMS_EPS)
    return (y * g.astype(jnp.float32)).astype(x.dtype)


def swiglu(h, w_up, w_down):
    gate, up = jnp.split(h @ w_up, 2, axis=-1)
    return (jax.nn.silu(gate) * up) @ w_down


def rotary(x):
    L, d = x.shape[1], x.shape[-1]
    half = d // 2
    inv_freq = ROPE_THETA ** (-jnp.arange(half, dtype=jnp.float32) / half)
    ang = jnp.arange(L, dtype=jnp.float32)[:, None] * inv_freq[None, :]
    cos = jnp.cos(ang)[None, :, None, :]
    sin = jnp.sin(ang)[None, :, None, :]
    x1, x2 = jnp.split(x.astype(jnp.float32), 2, axis=-1)
    return jnp.concatenate([x1 * cos - x2 * sin, x2 * cos + x1 * sin], axis=-1).astype(x.dtype)


def neighborhood_attention(q, k, v, rpb):
    bsz, L, H, d = q.shape
    rows = L // GRID_W
    kh = min(NA_KH, rows)
    r = jnp.arange(rows)
    row_idx = jnp.clip(r - kh // 2, 0, rows - kh)[:, None] + jnp.arange(kh)[None, :]
    col = jnp.arange(GRID_W)
    col_start = jnp.clip(col - NA_KW // 2, 0, GRID_W - NA_KW)
    col_ok = (col[None, :] >= col_start[:, None]) & (col[None, :] < col_start[:, None] + NA_KW)
    dc = jnp.clip(col[None, :] - col[:, None], -(NA_KW - 1), NA_KW - 1) + (NA_KW - 1)
    dr = row_idx - r[:, None] + (NA_KH - 1)
    qg = q.reshape(bsz, rows, GRID_W, H, d)
    kg = k.reshape(bsz, rows, GRID_W, H, d)[:, row_idx].reshape(bsz, rows, kh * GRID_W, H, d)
    vg = v.reshape(bsz, rows, GRID_W, H, d)[:, row_idx].reshape(bsz, rows, kh * GRID_W, H, d)
    s = jnp.einsum('brqhd,brkhd->brhqk', qg, kg).astype(jnp.float32) * (d ** -0.5)
    bias = rpb.astype(jnp.float32)[:, dr[:, None, :, None], dc[None, :, None, :]]
    bias = jnp.moveaxis(bias.reshape(H, rows, GRID_W, kh * GRID_W), 0, 1)
    mask = jnp.broadcast_to(col_ok[:, None, :], (GRID_W, kh, GRID_W)).reshape(GRID_W, kh * GRID_W)
    p = jax.nn.softmax(jnp.where(mask, s + bias, NEG_INF), axis=-1)
    o = jnp.einsum('brhqk,brkhd->brqhd', p.astype(v.dtype), vg)
    return o.reshape(bsz, L, H * d)


def _ssm_combine(e1, e2):
    a1, b1 = e1
    a2, b2 = e2
    return a2 * a1, a2 * b1 + b2


def s5_bidirectional(u, lam_re, lam_im, log_dt, b_re, b_im, c_re, c_im, d_skip, w_glu, b_glu):
    bsz, L, _ = u.shape
    f32 = jnp.float32
    lam = lax.complex(lam_re.astype(f32), lam_im.astype(f32))
    dt = jnp.exp(log_dt.astype(f32))[..., None]
    lam_bar = jnp.exp(lam * dt)
    b_mat = lax.complex(b_re.astype(f32), b_im.astype(f32))
    b_bar = ((lam_bar - 1.0) / lam)[..., None] * b_mat
    c_mat = lax.complex(c_re.astype(f32), c_im.astype(f32))
    ug = u.reshape(bsz, L, S5_GROUPS, S5_GROUP).astype(f32)
    ug_c = ug.astype(jnp.complex64)
    y = d_skip.astype(f32) * ug
    for s, rev in ((0, False), (1, True)):
        bu = jnp.einsum('blgh,gph->blgp', ug_c, b_bar[s])
        a = jnp.broadcast_to(lam_bar[s], bu.shape)
        _, st = lax.associative_scan(_ssm_combine, (a, bu), axis=1, reverse=rev)
        y = y + jnp.real(jnp.einsum('ghp,blgp->blgh', c_mat[s], st))
    z = jax.nn.gelu(y).reshape(bsz, L, BRANCH_W).astype(u.dtype)
    val, gate = jnp.split(z @ w_glu + b_glu, 2, axis=-1)
    return val * jax.nn.sigmoid(gate)


def diff_attention(q1, q2, k1, k2, v, lam_p, subln_g, lam_init):
    bsz, L, H, d = q1.shape
    q = jnp.stack([rotary(q1), rotary(q2)], axis=2)
    k = jnp.stack([rotary(k1), rotary(k2)], axis=2)
    lp = lam_p.astype(jnp.float32)
    lam = jnp.exp(jnp.sum(lp[0] * lp[1])) - jnp.exp(jnp.sum(lp[2] * lp[3])) + lam_init
    scale = d ** -0.5
    nblk = L // Q_BLOCK
    qb = jnp.moveaxis(q.reshape(bsz, nblk, Q_BLOCK, 2, H, d), 1, 0)

    def block(qblk):
        s = jnp.einsum('bqmhd,bkmhd->bmhqk', qblk, k).astype(jnp.float32) * scale
        p = jax.nn.softmax(s, axis=-1)
        attn = p[:, 0] - lam * p[:, 1]
        return jnp.einsum('bhqk,bkhe->bqhe', attn.astype(v.dtype), v)

    o = jnp.moveaxis(lax.map(block, qb), 0, 1).reshape(bsz, L, H, DIFF_V_DIM)
    o = rmsnorm(o, subln_g) * (1.0 - lam_init)
    return o.reshape(bsz, L, H * DIFF_V_DIM)


def token_mixer(h, w_in, w_branch, w_out, na_rpb, s5_lam_re, s5_lam_im, s5_log_dt, s5_b_re, s5_b_im,
                s5_c_re, s5_c_im, s5_d, s5_w_glu, s5_b_glu, diff_lam, diff_subln, lam_init):
    bsz, L, _ = h.shape
    widths = [BRANCH_W] * 4 + [DIFF_HEADS * DIFF_HEAD_DIM] * 4 + [BRANCH_W]
    offsets = []
    acc = 0
    for w in widths:
        acc += w
        offsets.append(acc)
    qa, ka, va, u, q1, q2, k1, k2, vc, gates = jnp.split(h @ w_in, offsets, axis=-1)
    na_heads = lambda t: t.reshape(bsz, L, NA_HEADS, NA_HEAD_DIM)
    df_heads = lambda t: t.reshape(bsz, L, DIFF_HEADS, -1)
    y_a = neighborhood_attention(na_heads(qa), na_heads(ka), na_heads(va), na_rpb)
    y_b = s5_bidirectional(u, s5_lam_re, s5_lam_im, s5_log_dt, s5_b_re, s5_b_im,
                           s5_c_re, s5_c_im, s5_d, s5_w_glu, s5_b_glu)
    y_c = diff_attention(df_heads(q1), df_heads(q2), df_heads(k1), df_heads(k2), df_heads(vc),
                         diff_lam, diff_subln, lam_init)
    ys = jnp.stack([y_a, y_b, y_c], axis=2)
    branch = jnp.einsum('blkw,kwd->blkd', ys, w_branch)
    g = jax.nn.sigmoid(gates.reshape(bsz, L, N_BRANCH, D_MODEL))
    merged = jnp.sum(g * branch, axis=2)
    return merged @ w_out


def modulate(x, g, shift, scale):
    return rmsnorm(x, g) * (1.0 + scale) + shift


def encoder_trunk(x, c, ada_w, ada_b, norm_pre, norm_post, ffn_w_in, ffn_w_out, w_in, w_branch, w_out,
                  na_rpb, s5_lam_re, s5_lam_im, s5_log_dt, s5_b_re, s5_b_im, s5_c_re, s5_c_im, s5_d,
                  s5_w_glu, s5_b_glu, diff_lam, diff_subln):
    bsz = x.shape[0]
    for l in range(DEPTH):
        lam_init = 0.8 - 0.6 * math.exp(-0.3 * l)
        mod = (jax.nn.silu(c) @ ada_w[l] + ada_b[l]).reshape(bsz, N_SUB, 3, D_MODEL)
        shift = mod[:, :, 0, None, :]
        scale = mod[:, :, 1, None, :]
        gate = mod[:, :, 2, None, :]
        h = modulate(x, norm_pre[l, 0], shift[:, 0], scale[:, 0])
        x = x + 0.5 * gate[:, 0] * rmsnorm(swiglu(h, ffn_w_in[l, 0], ffn_w_out[l, 0]), norm_post[l, 0])
        h = modulate(x, norm_pre[l, 1], shift[:, 1], scale[:, 1])
        m = token_mixer(h, w_in[l], w_branch[l], w_out[l], na_rpb[l], s5_lam_re[l], s5_lam_im[l],
                        s5_log_dt[l], s5_b_re[l], s5_b_im[l], s5_c_re[l], s5_c_im[l], s5_d[l],
                        s5_w_glu[l], s5_b_glu[l], diff_lam[l], diff_subln[l], lam_init)
        x = x + gate[:, 1] * rmsnorm(m, norm_post[l, 1])
        h = modulate(x, norm_pre[l, 2], shift[:, 2], scale[:, 2])
        x = x + 0.5 * gate[:, 2] * rmsnorm(swiglu(h, ffn_w_in[l, 1], ffn_w_out[l, 1]), norm_post[l, 2])
    return x


def setup_inputs(seed: int = 0) -> dict:
    key = jax.random.key(seed)
    ks = jax.random.split(key, 26)
    f32 = jnp.float32

    def nrm(k, shape, std):
        return std * jax.random.normal(k, shape, f32)

    G, P, HG = S5_GROUPS, S5_STATE, S5_GROUP
    return {
        "x_prompt": nrm(ks[0], (BATCH, SEQ, D_MODEL), 1.0),
        "x_sample": nrm(ks[1], (DEC_BATCH, DEC_SEQ, D_MODEL), 1.0),
        "c_prompt": nrm(ks[2], (BATCH, D_MODEL), 1.0),
        "c_sample": nrm(ks[3], (DEC_BATCH, D_MODEL), 1.0),
        "ada_w": nrm(ks[4], (DEPTH, D_MODEL, N_SUB * 3 * D_MODEL), 0.5 * D_MODEL ** -0.5),
        "ada_b": nrm(ks[5], (DEPTH, N_SUB * 3 * D_MODEL), 0.02),
        "norm_pre": 1.0 + nrm(ks[6], (DEPTH, N_SUB, D_MODEL), 0.05),
        "norm_post": 1.0 + nrm(ks[7], (DEPTH, N_SUB, D_MODEL), 0.05),
        "ffn_w_in": nrm(ks[8], (DEPTH, 2, D_MODEL, 2 * D_FF), D_MODEL ** -0.5),
        "ffn_w_out": nrm(ks[9], (DEPTH, 2, D_FF, D_MODEL), D_FF ** -0.5),
        "w_in": nrm(ks[10], (DEPTH, D_MODEL, IN_COLS), D_MODEL ** -0.5),
        "w_branch": nrm(ks[11], (DEPTH, N_BRANCH, BRANCH_W, D_MODEL), BRANCH_W ** -0.5),
        "w_out": nrm(ks[12], (DEPTH, D_MODEL, D_MODEL), D_MODEL ** -0.5),
        "na_rpb": nrm(ks[13], (DEPTH, NA_HEADS, 2 * NA_KH - 1, 2 * NA_KW - 1), 0.02),
        "s5_lam_re": -0.5 + nrm(ks[14], (DEPTH, 2, G, P), 0.01),
        "s5_lam_im": math.pi * jnp.arange(P, dtype=f32) + nrm(ks[15], (DEPTH, 2, G, P), 0.01),
        "s5_log_dt": jax.random.uniform(ks[16], (DEPTH, 2, G), f32, math.log(1e-3), math.log(1e-1)),
        "s5_b_re": nrm(ks[17], (DEPTH, 2, G, P, HG), (2 * HG) ** -0.5),
        "s5_b_im": nrm(ks[18], (DEPTH, 2, G, P, HG), (2 * HG) ** -0.5),
        "s5_c_re": nrm(ks[19], (DEPTH, 2, G, HG, P), (2 * P) ** -0.5),
        "s5_c_im": nrm(ks[20], (DEPTH, 2, G, HG, P), (2 * P) ** -0.5),
        "s5_d": nrm(ks[21], (DEPTH, G, HG), 1.0),
        "s5_w_glu": nrm(ks[22], (DEPTH, BRANCH_W, 2 * BRANCH_W), BRANCH_W ** -0.5),
        "s5_b_glu": nrm(ks[23], (DEPTH, 2 * BRANCH_W), 0.02),
        "diff_lam": nrm(ks[24], (DEPTH, 4, DIFF_HEAD_DIM), 0.1),
        "diff_subln": 1.0 + nrm(ks[25], (DEPTH, DIFF_V_DIM), 0.05),
    }


def reference(x_prompt, x_sample, c_prompt, c_sample, ada_w, ada_b, norm_pre, norm_post, ffn_w_in, ffn_w_out,
              w_in, w_branch, w_out, na_rpb, s5_lam_re, s5_lam_im, s5_log_dt, s5_b_re, s5_b_im, s5_c_re,
              s5_c_im, s5_d, s5_w_glu, s5_b_glu, diff_lam, diff_subln):
    y_prompt = encoder_trunk(x_prompt, c_prompt, ada_w, ada_b, norm_pre, norm_post, ffn_w_in, ffn_w_out,
                             w_in, w_branch, w_out, na_rpb, s5_lam_re, s5_lam_im, s5_log_dt, s5_b_re,
                             s5_b_im, s5_c_re, s5_c_im, s5_d, s5_w_glu, s5_b_glu, diff_lam, diff_subln)
    y_sample = encoder_trunk(x_sample, c_sample, ada_w, ada_b, norm_pre, norm_post, ffn_w_in, ffn_w_out,
                             w_in, w_branch, w_out, na_rpb, s5_lam_re, s5_lam_im, s5_log_dt, s5_b_re,
                             s5_b_im, s5_c_re, s5_c_im, s5_d, s5_w_glu, s5_b_glu, diff_lam, diff_subln)
    return (y_prompt, y_sample)
```

```python
import functools
import math

import jax
import jax.numpy as jnp
from jax import lax
from jax.experimental import pallas as pl
from jax.experimental.pallas import tpu as pltpu

F32 = jnp.float32
BF16 = jnp.bfloat16

GRID_W = 64
ROPE_THETA = 10000.0
RMS_EPS = 1e-6
NEG_INF = -1e30
LANES = 128
S5_CHUNK = 16
VMEM_LIMIT_BYTES = 56 * 1024 * 1024


def _params(ndims):
    return pltpu.CompilerParams(dimension_semantics=("arbitrary",) * ndims,
                                vmem_limit_bytes=VMEM_LIMIT_BYTES)


def _pick(n, *cands):
    for c in cands:
        if n % c == 0:
            return c
    return n


def _silu(x):
    return x * jax.nn.sigmoid(x)


def _gelu_tanh(x):
    return 0.5 * x * (1.0 + jnp.tanh(math.sqrt(2.0 / math.pi) * (x + 0.044715 * (x * x * x))))


def _ada_kernel(c_ref, w_ref, b_ref, o_ref, *, kc):
    d, nb = c_ref.shape
    tn = w_ref.shape[1]

    def body(k, accs):
        off = pl.multiple_of(k * kc, kc)
        wk = w_ref[pl.ds(off, kc), :]
        ck = _silu(c_ref[pl.ds(off, kc), :])
        return tuple(acc + jnp.sum(wk * ck[:, b:b + 1], axis=0, keepdims=True)
                     for b, acc in enumerate(accs))

    accs = lax.fori_loop(0, d // kc, body, tuple(jnp.zeros((1, tn), F32) for _ in range(nb)))
    for b in range(nb):
        o_ref[b:b + 1, :] = accs[b] + b_ref[...]


def _ada_mod(c_all, ada_w, ada_b):
    nb, d = c_all.shape
    depth, _, n = ada_w.shape
    tn = _pick(n, 512, 256, 128)
    kc = _pick(d, 256, 128, 8)
    return pl.pallas_call(
        functools.partial(_ada_kernel, kc=kc),
        grid=(depth, n // tn),
        in_specs=[pl.BlockSpec((d, nb), lambda l, j: (0, 0)),
                  pl.BlockSpec((None, d, tn), lambda l, j: (l, 0, j)),
                  pl.BlockSpec((None, 1, tn), lambda l, j: (l, 0, j))],
        out_specs=pl.BlockSpec((None, nb, tn), lambda l, j: (l, 0, j)),
        out_shape=jax.ShapeDtypeStruct((depth, nb, n), F32),
        compiler_params=_params(2),
        name="ada_mod",
    )(c_all.T, ada_w, ada_b.reshape(depth, 1, n))


def _modulate_kernel(x_ref, g_ref, sc_ref, sh_ref, o_ref):
    x = x_ref[...]
    y = x * lax.rsqrt(jnp.mean(x * x, axis=-1, keepdims=True) + RMS_EPS) * g_ref[...]
    o_ref[...] = (y * (1.0 + sc_ref[...]) + sh_ref[...]).astype(o_ref.dtype)


def _modulate(x, g, scale, shift):
    m, d = x.shape
    tm = _pick(m, 256, 128, 8)
    row = pl.BlockSpec((1, d), lambda i: (0, 0))
    return pl.pallas_call(
        _modulate_kernel,
        grid=(m // tm,),
        in_specs=[pl.BlockSpec((tm, d), lambda i: (i, 0)), row, row, row],
        out_specs=pl.BlockSpec((tm, d), lambda i: (i, 0)),
        out_shape=jax.ShapeDtypeStruct((m, d), BF16),
        compiler_params=_params(1),
        name="modulate",
    )(x, g.reshape(1, d), scale.reshape(1, d), shift.reshape(1, d))


def _residual_kernel(x_ref, y_ref, g_ref, gate_ref, o_ref, *, coef):
    y = y_ref[...]
    n = y * lax.rsqrt(jnp.mean(y * y, axis=-1, keepdims=True) + RMS_EPS) * g_ref[...]
    o_ref[...] = x_ref[...] + (coef * gate_ref[...]) * n


def _residual(x, y, g, gate, coef):
    m, d = x.shape
    tm = _pick(m, 256, 128, 8)
    row = pl.BlockSpec((1, d), lambda i: (0, 0))
    blk = pl.BlockSpec((tm, d), lambda i: (i, 0))
    return pl.pallas_call(
        functools.partial(_residual_kernel, coef=coef),
        grid=(m // tm,),
        in_specs=[blk, blk, row, row],
        out_specs=blk,
        out_shape=jax.ShapeDtypeStruct((m, d), F32),
        compiler_params=_params(1),
        name="residual",
    )(x, y, g.reshape(1, d), gate.reshape(1, d))


def _mm_kernel(a_ref, w_ref, o_ref):
    o_ref[...] = jnp.dot(a_ref[...], w_ref[...], preferred_element_type=F32).astype(o_ref.dtype)


def _mm(a, w, lead, col_off, ncols, out_dtype, tm_pref=1024, tn_pref=1024, name="mm"):
    m, k = a.shape
    tm = _pick(m, *[c for c in (1024, 512, 256, 128, 8) if c <= tm_pref])
    tn = _pick(math.gcd(col_off, ncols), *[c for c in (1024, 768, 512, 384, 256, 128) if c <= tn_pref])
    off = col_off // tn
    w_spec = pl.BlockSpec((None,) * len(lead) + (k, tn), lambda i, j: tuple(lead) + (0, off + j))
    return pl.pallas_call(
        _mm_kernel,
        grid=(m // tm, ncols // tn),
        in_specs=[pl.BlockSpec((tm, k), lambda i, j: (i, 0)), w_spec],
        out_specs=pl.BlockSpec((tm, tn), lambda i, j: (i, j)),
        out_shape=jax.ShapeDtypeStruct((m, ncols), out_dtype),
        compiler_params=_params(2),
        name=name,
    )(a, w)


def _mm_glu_kernel(a_ref, w1_ref, w2_ref, b1_ref, b2_ref, o_ref, *, swiglu):
    a = a_ref[...]
    y1 = jnp.dot(a, w1_ref[...], preferred_element_type=F32) + b1_ref[...]
    y2 = jnp.dot(a, w2_ref[...], preferred_element_type=F32) + b2_ref[...]
    out = _silu(y1) * y2 if swiglu else y1 * jax.nn.sigmoid(y2)
    o_ref[...] = out.astype(o_ref.dtype)


def _mm_glu(a, w, lead, bias, swiglu, name):
    m, k = a.shape
    half = w.shape[-1] // 2
    tm = _pick(m, 1024, 512, 256, 128, 8)
    tn = _pick(half, 512, 256, 128)
    nj = half // tn
    nl = len(lead)
    w1 = pl.BlockSpec((None,) * nl + (k, tn), lambda i, j: tuple(lead) + (0, j))
    w2 = pl.BlockSpec((None,) * nl + (k, tn), lambda i, j: tuple(lead) + (0, nj + j))
    b1 = pl.BlockSpec((1, tn), lambda i, j: (0, j))
    b2 = pl.BlockSpec((1, tn), lambda i, j: (0, nj + j))
    return pl.pallas_call(
        functools.partial(_mm_glu_kernel, swiglu=swiglu),
        grid=(m // tm, nj),
        in_specs=[pl.BlockSpec((tm, k), lambda i, j: (i, 0)), w1, w2, b1, b2],
        out_specs=pl.BlockSpec((tm, tn), lambda i, j: (i, j)),
        out_shape=jax.ShapeDtypeStruct((m, half), BF16),
        compiler_params=_params(2),
        name=name,
    )(a, w, w, bias, bias)


def _merge_kernel(ya_ref, yb_ref, yc_ref, w_ref, ga_ref, gb_ref, gc_ref, o_ref):
    acc = jax.nn.sigmoid(ga_ref[...]) * jnp.dot(ya_ref[...], w_ref[0], preferred_element_type=F32)
    acc += jax.nn.sigmoid(gb_ref[...]) * jnp.dot(yb_ref[...], w_ref[1], preferred_element_type=F32)
    acc += jax.nn.sigmoid(gc_ref[...]) * jnp.dot(yc_ref[...], w_ref[2], preferred_element_type=F32)
    o_ref[...] = acc.astype(o_ref.dtype)


def _merge(ya, yb, yc, w_branch, l, gates):
    m, bw = ya.shape
    d = w_branch.shape[-1]
    tm = _pick(m, 512, 256, 128, 8)
    tn = _pick(d, 512, 256, 128)
    nj = d // tn
    y_spec = pl.BlockSpec((tm, bw), lambda i, j: (i, 0))
    g_specs = [pl.BlockSpec((tm, tn), functools.partial(lambda i, j, k: (i, k * nj + j), k=k))
               for k in range(3)]
    return pl.pallas_call(
        _merge_kernel,
        grid=(m // tm, nj),
        in_specs=[y_spec, y_spec, y_spec,
                  pl.BlockSpec((None, 3, bw, tn), lambda i, j: (l, 0, 0, j))] + g_specs,
        out_specs=pl.BlockSpec((tm, tn), lambda i, j: (i, j)),
        out_shape=jax.ShapeDtypeStruct((m, d), BF16),
        compiler_params=_params(2),
        name="branch_merge",
    )(ya, yb, yc, w_branch, gates, gates, gates)


def _na_bias_kernel(rpb_ref, o_ref, *, kw, nr, nc):
    h = pl.program_id(0)
    w = GRID_W
    q = lax.broadcasted_iota(jnp.int32, (w, 2 * w), 0)
    c2 = lax.broadcasted_iota(jnp.int32, (w, 2 * w), 1)
    second = c2 >= w
    c = jnp.where(second, c2 - w, c2)
    dc = jnp.clip(c - q, -(kw - 1), kw - 1) + (kw - 1)
    cs = jnp.clip(q - kw // 2, 0, w - kw)
    ok = jnp.logical_and(c >= cs, c < cs + kw)
    for dr in range(nr - 1):
        t = jnp.zeros((w, 2 * w), F32)
        for dd in range(nc):
            lo = rpb_ref[(h * nr + dr) * nc + dd]
            hi = rpb_ref[(h * nr + dr + 1) * nc + dd]
            t = jnp.where(dc == dd, jnp.where(second, hi, lo), t)
        o_ref[dr] = jnp.where(ok, t, NEG_INF)


def _na_bias(rpb, kw):
    heads, nr, nc = rpb.shape
    return pl.pallas_call(
        functools.partial(_na_bias_kernel, kw=kw, nr=nr, nc=nc),
        grid=(heads,),
        in_specs=[pl.BlockSpec(memory_space=pltpu.SMEM)],
        out_specs=pl.BlockSpec((None, nr - 1, GRID_W, 2 * GRID_W), lambda h: (h, 0, 0, 0)),
        out_shape=jax.ShapeDtypeStruct((heads, nr - 1, GRID_W, 2 * GRID_W), F32),
        compiler_params=_params(1),
        name="na_bias",
    )(rpb.reshape(-1))


def _na_kernel(q_ref, k_ref, v_ref, tab_ref, o_ref, *, rb, rows, kh, kh_full, scale):
    i = pl.program_id(1)
    w = GRID_W
    for rr in range(rb):
        r = i * rb + rr
        start = jnp.clip(r - kh // 2, 0, rows - kh)
        dr0 = start - r + (kh_full - 1)
        tok = pl.multiple_of(start * w, w)
        q = q_ref[rr * w:(rr + 1) * w, :]
        kwin = k_ref[pl.ds(tok, kh * w), :]
        vwin = v_ref[pl.ds(tok, kh * w), :]
        s = lax.dot_general(q, kwin, (((1,), (1,)), ((), ())), preferred_element_type=F32) * scale
        bias = jnp.concatenate([tab_ref[dr0 + 2 * mm] for mm in range(kh // 2)], axis=1)
        s = s + bias
        p = jnp.exp(s - jnp.max(s, axis=-1, keepdims=True))
        denom = jnp.sum(p, axis=-1, keepdims=True)
        o = jnp.dot(p.astype(BF16), vwin, preferred_element_type=F32)
        o_ref[rr * w:(rr + 1) * w, :] = (o / denom).astype(o_ref.dtype)


def _neighborhood_attention(qkv, rpb, kw):
    L = qkv.shape[0]
    heads, nr, _ = rpb.shape
    bw = qkv.shape[1] // 3
    hd = bw // heads
    assert hd == LANES, "neighbourhood-attention head dim must fill one lane tile"
    rows = L // GRID_W
    kh_full = (nr + 1) // 2
    kh = min(kh_full, rows)
    assert kh % 2 == 0
    tab = _na_bias(rpb, kw)
    rb = _pick(rows, 8, 4, 2, 1)
    return pl.pallas_call(
        functools.partial(_na_kernel, rb=rb, rows=rows, kh=kh, kh_full=kh_full, scale=hd ** -0.5),
        grid=(heads, rows // rb),
        in_specs=[pl.BlockSpec((rb * GRID_W, hd), lambda h, i: (i, h)),
                  pl.BlockSpec((L, hd), lambda h, i: (0, heads + h)),
                  pl.BlockSpec((L, hd), lambda h, i: (0, 2 * heads + h)),
                  pl.BlockSpec((None, nr - 1, GRID_W, 2 * GRID_W), lambda h, i: (h, 0, 0, 0))],
        out_specs=pl.BlockSpec((rb * GRID_W, hd), lambda h, i: (i, h)),
        out_shape=jax.ShapeDtypeStruct((L, bw), BF16),
        compiler_params=_params(2),
        name="neighborhood_attention",
    )(qkv, qkv, qkv, tab)


def _rotary_kernel(x_ref, cos_ref, sin_ref, o_ref, *, hd, n_q_blocks, q_scale):
    j = pl.program_id(1)
    x = x_ref[...]
    lane = lax.broadcasted_iota(jnp.int32, x.shape, 1)
    first = (lane % hd) < (hd // 2)
    rot = jnp.where(first, pltpu.roll(x, LANES - hd // 2, 1), pltpu.roll(x, hd // 2, 1))
    y = x * cos_ref[...] + rot * sin_ref[...]
    y = y * jnp.where(j < n_q_blocks, q_scale, 1.0)
    for t in range(LANES // hd):
        o_ref[t] = y[:, t * hd:(t + 1) * hd].astype(o_ref.dtype)


def _rotary(qk, hd):
    L, cols = qk.shape
    assert LANES % hd == 0
    half = hd // 2
    inv_freq = ROPE_THETA ** (-jnp.arange(half, dtype=F32) / half)
    ang = jnp.arange(L, dtype=F32)[:, None] * inv_freq[None, :]
    cos, sin = jnp.cos(ang), jnp.sin(ang)
    reps = LANES // hd
    cos_t = jnp.tile(jnp.concatenate([cos, cos], axis=1), (1, reps))
    sin_t = jnp.tile(jnp.concatenate([-sin, sin], axis=1), (1, reps))
    tm = _pick(L, 512, 256, 128, 8)
    nblk = cols // LANES
    return pl.pallas_call(
        functools.partial(_rotary_kernel, hd=hd, n_q_blocks=nblk // 2, q_scale=hd ** -0.5),
        grid=(L // tm, nblk),
        in_specs=[pl.BlockSpec((tm, LANES), lambda i, j: (i, j)),
                  pl.BlockSpec((tm, LANES), lambda i, j: (i, 0)),
                  pl.BlockSpec((tm, LANES), lambda i, j: (i, 0))],
        out_specs=pl.BlockSpec((reps, tm, hd), lambda i, j: (j, i, 0)),
        out_shape=jax.ShapeDtypeStruct((cols // hd, L, hd), BF16),
        compiler_params=_params(2),
        name="rotary",
    )(qk, cos_t, sin_t)


def _diff_attn_kernel(q1_ref, q2_ref, k1_ref, k2_ref, v_ref, lam_ref, g_ref, o_ref,
                      m1, l1, a1, m2, l2, a2, *, tk, lam_init):
    nk = k1_ref.shape[0]
    for m_sc, l_sc, a_sc in ((m1, l1, a1), (m2, l2, a2)):
        m_sc[...] = jnp.full(m_sc.shape, NEG_INF, F32)
        l_sc[...] = jnp.zeros(l_sc.shape, F32)
        a_sc[...] = jnp.zeros(a_sc.shape, F32)

    def body(c, carry):
        vv = v_ref[pl.ds(pl.multiple_of(c * tk, tk), tk), :]
        for q_ref, k_ref, m_sc, l_sc, a_sc in ((q1_ref, k1_ref, m1, l1, a1), (q2_ref, k2_ref, m2, l2, a2)):
            s = jnp.dot(q_ref[...], k_ref[c], preferred_element_type=F32)
            m_old = m_sc[...]
            m_new = jnp.maximum(m_old, jnp.max(s, axis=-1, keepdims=True))
            p = jnp.exp(s - m_new)
            alpha = jnp.exp(m_old - m_new)
            l_sc[...] = alpha * l_sc[...] + jnp.sum(p, axis=-1, keepdims=True)
            a_sc[...] = alpha * a_sc[...] + jnp.dot(p.astype(BF16), vv, preferred_element_type=F32)
            m_sc[...] = m_new
        return carry

    lax.fori_loop(0, nk, body, 0)
    lp = lam_ref[...]
    lam = (jnp.exp(jnp.sum(lp[0:1] * lp[1:2], axis=-1, keepdims=True))
           - jnp.exp(jnp.sum(lp[2:3] * lp[3:4], axis=-1, keepdims=True)) + lam_init)
    o = a1[...] / l1[...] - lam * (a2[...] / l2[...])
    o = o * lax.rsqrt(jnp.mean(o * o, axis=-1, keepdims=True) + RMS_EPS) * g_ref[...]
    o_ref[...] = (o * (1.0 - lam_init)).astype(o_ref.dtype)


def _diff_attention(qk_rot, v, diff_lam, subln_g, lam_init):
    h4, L, hd = qk_rot.shape
    heads = h4 // 4
    vd = v.shape[1] // heads
    assert vd == LANES, "differential-attention value dim must fill one lane tile"
    tq = _pick(L, 512, 256, 128, 8)
    tk = _pick(L, 512, 256, 128, 8)
    nk = L // tk
    kt = jnp.swapaxes(qk_rot[2 * heads:].reshape(2 * heads, nk, tk, hd), 2, 3)
    stat = pltpu.VMEM((tq, 1), F32)
    acc = pltpu.VMEM((tq, vd), F32)
    return pl.pallas_call(
        functools.partial(_diff_attn_kernel, tk=tk, lam_init=lam_init),
        grid=(heads, L // tq),
        in_specs=[pl.BlockSpec((None, tq, hd), lambda h, i: (h, i, 0)),
                  pl.BlockSpec((None, tq, hd), lambda h, i: (heads + h, i, 0)),
                  pl.BlockSpec((None, nk, hd, tk), lambda h, i: (h, 0, 0, 0)),
                  pl.BlockSpec((None, nk, hd, tk), lambda h, i: (heads + h, 0, 0, 0)),
                  pl.BlockSpec((L, vd), lambda h, i: (0, h)),
                  pl.BlockSpec(diff_lam.shape, lambda h, i: (0, 0)),
                  pl.BlockSpec((1, vd), lambda h, i: (0, 0))],
        out_specs=pl.BlockSpec((tq, vd), lambda h, i: (i, h)),
        out_shape=jax.ShapeDtypeStruct((L, heads * vd), BF16),
        scratch_shapes=[stat, stat, acc, stat, stat, acc],
        compiler_params=_params(2),
        name="diff_attention",
    )(qk_rot, qk_rot, kt, kt, v, diff_lam, subln_g.reshape(1, vd))


def _s5_tables(lam_re, lam_im, log_dt, b_re, b_im, c_re, c_im, d_skip):
    T = S5_CHUNK
    _, G, P, HG = b_re.shape
    dt = jnp.exp(log_dt.astype(F32))[..., None]
    lr, li = lam_re.astype(F32), lam_im.astype(F32)
    j = jnp.arange(T + 1, dtype=F32)[:, None, None, None]
    mag = jnp.exp(lr * dt * j)
    pw_re, pw_im = mag * jnp.cos(li * dt * j), mag * jnp.sin(li * dt * j)
    den = lr * lr + li * li
    nr_, ni_ = pw_re[1] - 1.0, pw_im[1]
    f_re, f_im = (nr_ * lr + ni_ * li) / den, (ni_ * lr - nr_ * li) / den
    bb_re = f_re[..., None] * b_re - f_im[..., None] * b_im
    bb_im = f_re[..., None] * b_im + f_im[..., None] * b_re
    cr, ci = c_re.astype(F32)[None], c_im.astype(F32)[None]
    cp_re = cr * pw_re[:, :, :, None, :] - ci * pw_im[:, :, :, None, :]
    cp_im = cr * pw_im[:, :, :, None, :] + ci * pw_re[:, :, :, None, :]
    hp = lax.Precision.HIGHEST
    kj = (jnp.einsum('jdghp,dgpk->jdghk', cp_re[:T], bb_re, precision=hp)
          - jnp.einsum('jdghp,dgpk->jdghk', cp_im[:T], bb_im, precision=hp))
    sig = jnp.arange(T)[:, None]
    tau = jnp.arange(T)[None, :]
    lag_f = jnp.clip(tau - sig, 0, T - 1)
    lag_b = jnp.clip(sig - tau, 0, T - 1)
    kf = jnp.where((tau >= sig)[:, :, None, None, None], kj[lag_f, 0], 0.0)
    kb = jnp.where((sig >= tau)[:, :, None, None, None], kj[lag_b, 1], 0.0)
    kin = jnp.transpose(kf + kb, (2, 0, 4, 1, 3)).reshape(G, T * HG, T * HG)
    def bend_dir(d, order):
        pr, pi = pw_re[order, d], pw_im[order, d]
        re = pr[..., None] * bb_re[d][None] - pi[..., None] * bb_im[d][None]
        im = pr[..., None] * bb_im[d][None] + pi[..., None] * bb_re[d][None]
        to_rows = lambda t: jnp.transpose(t, (1, 0, 3, 2)).reshape(G, T * HG, P)
        return to_rows(re), to_rows(im)
    bf_re, bf_im = bend_dir(0, jnp.arange(T - 1, -1, -1))
    bb_re_, bb_im_ = bend_dir(1, jnp.arange(T))
    bend = jnp.stack([bf_re, bf_im, bb_re_, bb_im_], axis=2)
    def wout_dir(d, order):
        to_cols = lambda t: jnp.transpose(t, (1, 3, 0, 2)).reshape(G, P, T * HG)
        return to_cols(cp_re[order, d]), to_cols(-cp_im[order, d])
    wf_re, wf_im = wout_dir(0, jnp.arange(1, T + 1))
    wb_re, wb_im = wout_dir(1, jnp.arange(T, 0, -1))
    wout = jnp.stack([wf_re, wf_im, wb_re, wb_im], axis=1)
    eye2 = jnp.eye(2, dtype=F32)
    G2 = G // 2
    kin2 = jnp.einsum('karc,ab->karbc', kin.reshape(G2, 2, T * HG, T * HG), eye2)
    kin2 = kin2.reshape(G2, 2 * T * HG, 2 * T * HG)
    bend2 = jnp.einsum('karqp,ab->karqbp', bend.reshape(G2, 2, T * HG, 4, P), eye2)
    bend2 = bend2.reshape(G2, 2 * T * HG, 4 * 2 * P)
    wout2 = jnp.einsum('kaqpc,ab->kqapbc', wout.reshape(G2, 2, 4, P, T * HG), eye2)
    wout2 = wout2.reshape(G2, 4 * 2 * P, 2 * T * HG)
    a = jnp.stack([pw_re[T, 0], pw_im[T, 0], pw_re[T, 1], pw_im[T, 1]], axis=1)
    a2 = jnp.transpose(a.reshape(G2, 2, 4, P), (0, 2, 1, 3)).reshape(G2, 4, 2 * P)
    dsk = jnp.tile(d_skip.astype(F32)[:, None, :], (1, T, 1)).reshape(G2, 1, 2 * T * HG)
    return kin2.astype(BF16), bend2.astype(BF16), wout2.astype(BF16), a2, dsk


def _s5_kernel(u_ref, kin_ref, bend_ref, wout_ref, a_ref, d_ref, o_ref, z_sc, s_sc):
    nc = u_ref.shape[0]
    sw = a_ref.shape[1]
    u = u_ref[...]
    ub = u.astype(BF16)
    z_sc[...] = jnp.dot(ub, bend_ref[...], preferred_element_type=F32)
    sub = 8
    nt = nc // sub
    afr, afi, abr, abi = (jnp.broadcast_to(a_ref[t:t + 1, :], (sub, sw)) for t in range(4))
    row = lax.broadcasted_iota(jnp.int32, (sub, sw), 0)

    def body(i, carry):
        xfr, xfi, xbr, xbi = carry
        rf = pl.multiple_of(i * sub, sub)
        rb = pl.multiple_of((nt - 1 - i) * sub, sub)
        zf = z_sc[pl.ds(rf, sub), 0:2 * sw]
        zb = z_sc[pl.ds(rb, sub), 2 * sw:4 * sw]
        sfr = sfi = sbr = sbi = jnp.zeros((sub, sw), F32)
        for jf in range(sub):
            jb = sub - 1 - jf
            sfr = jnp.where(row == jf, xfr, sfr)
            sfi = jnp.where(row == jf, xfi, sfi)
            sbr = jnp.where(row == jb, xbr, sbr)
            sbi = jnp.where(row == jb, xbi, sbi)
            zfj = jnp.broadcast_to(zf[jf:jf + 1, :], (sub, 2 * sw))
            zbj = jnp.broadcast_to(zb[jb:jb + 1, :], (sub, 2 * sw))
            xfr, xfi = (afr * xfr - afi * xfi + zfj[:, 0:sw], afr * xfi + afi * xfr + zfj[:, sw:2 * sw])
            xbr, xbi = (abr * xbr - abi * xbi + zbj[:, 0:sw], abr * xbi + abi * xbr + zbj[:, sw:2 * sw])
        s_sc[pl.ds(rf, sub), 0:sw] = sfr
        s_sc[pl.ds(rf, sub), sw:2 * sw] = sfi
        s_sc[pl.ds(rb, sub), 2 * sw:3 * sw] = sbr
        s_sc[pl.ds(rb, sub), 3 * sw:4 * sw] = sbi
        return xfr, xfi, xbr, xbi

    zero = jnp.zeros((sub, sw), F32)
    lax.fori_loop(0, nt, body, (zero, zero, zero, zero))
    y = jnp.dot(ub, kin_ref[...], preferred_element_type=F32)
    y += jnp.dot(s_sc[...].astype(BF16), wout_ref[...], preferred_element_type=F32)
    y += d_ref[...] * u
    o_ref[...] = _gelu_tanh(y).astype(o_ref.dtype)


def _s5_bidirectional(u, tables):
    kin, bend, wout, a, dsk = tables
    L, bw = u.shape
    T = S5_CHUNK
    g2 = kin.shape[0]
    cw = kin.shape[1]
    hg = cw // (2 * T)
    nc = L // T
    assert L % T == 0 and nc % 8 == 0 and g2 * 2 * hg == bw
    up = jnp.transpose(u.reshape(nc, T, g2, 2, hg), (0, 2, 3, 1, 4)).reshape(nc, g2 * cw)
    sw4 = bend.shape[2]
    z = pl.pallas_call(
        _s5_kernel,
        grid=(g2,),
        in_specs=[pl.BlockSpec((nc, cw), lambda g: (0, g)),
                  pl.BlockSpec((None, cw, cw), lambda g: (g, 0, 0)),
                  pl.BlockSpec((None, cw, sw4), lambda g: (g, 0, 0)),
                  pl.BlockSpec((None, sw4, cw), lambda g: (g, 0, 0)),
                  pl.BlockSpec((None, 4, sw4 // 4), lambda g: (g, 0, 0)),
                  pl.BlockSpec((None, 1, cw), lambda g: (g, 0, 0))],
        out_specs=pl.BlockSpec((nc, cw), lambda g: (0, g)),
        out_shape=jax.ShapeDtypeStruct((nc, g2 * cw), BF16),
        scratch_shapes=[pltpu.VMEM((nc, sw4), F32), pltpu.VMEM((nc, sw4), F32)],
        compiler_params=_params(1),
        name="s5_scan",
    )(up, kin, bend, wout, a, dsk)
    return jnp.transpose(z.reshape(nc, g2, 2, T, hg), (0, 3, 1, 2, 4)).reshape(L, bw)


def _encoder_trunk(x, mod, wts, prm):
    L, d = x.shape
    depth = wts["ffn_w_in"].shape[0]
    bw = wts["w_branch"].shape[2]
    hd = prm["diff_lam"].shape[-1]
    kw = (prm["na_rpb"].shape[-1] + 1) // 2
    qk_cols = wts["w_in"].shape[-1] - 5 * bw - 3 * d
    zero_bias = jnp.zeros((1, wts["ffn_w_in"].shape[-1]), F32)

    def mod_rows(l, sub):
        return tuple(mod[l, (3 * sub + t) * d:(3 * sub + t + 1) * d] for t in range(3))

    def ffn(x, l, which, sub):
        shift, scale, gate = mod_rows(l, sub)
        h = _modulate(x, prm["norm_pre"][l, sub], scale, shift)
        act = _mm_glu(h, wts["ffn_w_in"], (l, which), zero_bias, True, "ffn_up")
        y = _mm(act, wts["ffn_w_out"], (l, which), 0, d, F32, tm_pref=512, tn_pref=256, name="ffn_down")
        return _residual(x, y, prm["norm_post"][l, sub], gate, 0.5)

    for l in range(depth):
        lam_init = 0.8 - 0.6 * math.exp(-0.3 * l)
        x = ffn(x, l, 0, 0)
        shift, scale, gate = mod_rows(l, 1)
        h = _modulate(x, prm["norm_pre"][l, 1], scale, shift)
        w_in = wts["w_in"]
        qkv_a = _mm(h, w_in, (l,), 0, 3 * bw, BF16, name="in_proj_na")
        u = _mm(h, w_in, (l,), 3 * bw, bw, F32, name="in_proj_s5")
        qk = _mm(h, w_in, (l,), 4 * bw, qk_cols, F32, name="in_proj_qk")
        vc = _mm(h, w_in, (l,), 4 * bw + qk_cols, bw, BF16, name="in_proj_v")
        gates = _mm(h, w_in, (l,), 5 * bw + qk_cols, 3 * d, F32, name="in_proj_gates")
        y_a = _neighborhood_attention(qkv_a, prm["na_rpb"][l], kw)
        z = _s5_bidirectional(u, prm["s5_tables"][l])
        y_b = _mm_glu(z, wts["s5_w_glu"], (l,), prm["s5_b_glu"][l].reshape(1, -1), False, "s5_glu")
        y_c = _diff_attention(_rotary(qk, hd), vc, prm["diff_lam"][l], prm["diff_subln"][l], lam_init)
        merged = _merge(y_a, y_b, y_c, wts["w_branch"], l, gates)
        m = _mm(merged, wts["w_out"], (l,), 0, d, F32, name="out_proj")
        x = _residual(x, m, prm["norm_post"][l, 1], gate, 1.0)
        x = ffn(x, l, 1, 2)
    return x


def kernel(x_prompt, x_sample, c_prompt, c_sample, ada_w, ada_b, norm_pre, norm_post, ffn_w_in, ffn_w_out,
           w_in, w_branch, w_out, na_rpb, s5_lam_re, s5_lam_im, s5_log_dt, s5_b_re, s5_b_im, s5_c_re,
           s5_c_im, s5_d, s5_w_glu, s5_b_glu, diff_lam, diff_subln):
    depth = ada_w.shape[0]
    wts = {"ffn_w_in": ffn_w_in.astype(BF16), "ffn_w_out": ffn_w_out.astype(BF16),
           "w_in": w_in.astype(BF16), "w_branch": w_branch.astype(BF16),
           "w_out": w_out.astype(BF16), "s5_w_glu": s5_w_glu.astype(BF16)}
    prm = {"norm_pre": norm_pre, "norm_post": norm_post, "na_rpb": na_rpb, "s5_b_glu": s5_b_glu,
           "diff_lam": diff_lam, "diff_subln": diff_subln,
           "s5_tables": [_s5_tables(s5_lam_re[l], s5_lam_im[l], s5_log_dt[l], s5_b_re[l], s5_b_im[l],
                                    s5_c_re[l], s5_c_im[l], s5_d[l]) for l in range(depth)]}
    c_all = jnp.concatenate([c_prompt, c_sample], axis=0)
    mod = _ada_mod(c_all, ada_w, ada_b)
    outs = []
    row = 0
    for x in (x_prompt, x_sample):
        ys = []
        for b in range(x.shape[0]):
            ys.append(_encoder_trunk(x[b], mod[:, row], wts, prm))
            row += 1
        outs.append(ys[0][None] if len(ys) == 1 else jnp.stack(ys, axis=0))
    return tuple(outs)
```

```python
import functools
import math

import jax
import jax.numpy as jnp
from jax import lax
from jax.experimental import pallas as pl
from jax.experimental.pallas import tpu as pltpu

F32 = jnp.float32
BF16 = jnp.bfloat16

GRID_W = 64
ROPE_THETA = 10000.0
RMS_EPS = 1e-6
NEG_INF = -1e30
LANES = 128
S5_CHUNK = 16
VMEM_LIMIT_BYTES = 56 * 1024 * 1024


def _params(ndims):
    return pltpu.CompilerParams(dimension_semantics=("arbitrary",) * ndims,
                                vmem_limit_bytes=VMEM_LIMIT_BYTES)


def _pick(n, *cands):
    for c in cands:
        if n % c == 0:
            return c
    return n


def _silu(x):
    return x * jax.nn.sigmoid(x)


def _gelu_tanh(x):
    return 0.5 * x * (1.0 + jnp.tanh(math.sqrt(2.0 / math.pi) * (x + 0.044715 * (x * x * x))))


def _ada_kernel(c_ref, w_ref, b_ref, o_ref, *, kc):
    d, nb = c_ref.shape
    tn = w_ref.shape[1]

    def body(k, accs):
        off = pl.multiple_of(k * kc, kc)
        wk = w_ref[pl.ds(off, kc), :]
        ck = _silu(c_ref[pl.ds(off, kc), :])
        return tuple(acc + jnp.sum(wk * ck[:, b:b + 1], axis=0, keepdims=True)
                     for b, acc in enumerate(accs))

    accs = lax.fori_loop(0, d // kc, body, tuple(jnp.zeros((1, tn), F32) for _ in range(nb)))
    for b in range(nb):
        o_ref[b:b + 1, :] = accs[b] + b_ref[...]


def _ada_mod(c_all, ada_w, ada_b):
    nb, d = c_all.shape
    depth, _, n = ada_w.shape
    tn = _pick(n, 512, 256, 128)
    kc = _pick(d, 256, 128, 8)
    return pl.pallas_call(
        functools.partial(_ada_kernel, kc=kc),
        grid=(depth, n // tn),
        in_specs=[pl.BlockSpec((d, nb), lambda l, j: (0, 0)),
                  pl.BlockSpec((None, d, tn), lambda l, j: (l, 0, j)),
                  pl.BlockSpec((None, 1, tn), lambda l, j: (l, 0, j))],
        out_specs=pl.BlockSpec((None, nb, tn), lambda l, j: (l, 0, j)),
        out_shape=jax.ShapeDtypeStruct((depth, nb, n), F32),
        compiler_params=_params(2),
        name="ada_mod",
    )(c_all.T, ada_w, ada_b.reshape(depth, 1, n))


def _modulate_kernel(x_ref, g_ref, sc_ref, sh_ref, o_ref):
    x = x_ref[...]
    y = x * lax.rsqrt(jnp.mean(x * x, axis=-1, keepdims=True) + RMS_EPS) * g_ref[...]
    o_ref[...] = (y * (1.0 + sc_ref[...]) + sh_ref[...]).astype(o_ref.dtype)


def _modulate(x, g, scale, shift):
    m, d = x.shape
    tm = _pick(m, 256, 128, 8)
    row = pl.BlockSpec((1, d), lambda i: (0, 0))
    return pl.pallas_call(
        _modulate_kernel,
        grid=(m // tm,),
        in_specs=[pl.BlockSpec((tm, d), lambda i: (i, 0)), row, row, row],
        out_specs=pl.BlockSpec((tm, d), lambda i: (i, 0)),
        out_shape=jax.ShapeDtypeStruct((m, d), BF16),
        compiler_params=_params(1),
        name="modulate",
    )(x, g.reshape(1, d), scale.reshape(1, d), shift.reshape(1, d))


def _residual_kernel(x_ref, y_ref, g_ref, gate_ref, o_ref, *, coef):
    y = y_ref[...]
    n = y * lax.rsqrt(jnp.mean(y * y, axis=-1, keepdims=True) + RMS_EPS) * g_ref[...]
    o_ref[...] = x_ref[...] + (coef * gate_ref[...]) * n


def _residual(x, y, g, gate, coef):
    m, d = x.shape
    tm = _pick(m, 256, 128, 8)
    row = pl.BlockSpec((1, d), lambda i: (0, 0))
    blk = pl.BlockSpec((tm, d), lambda i: (i, 0))
    return pl.pallas_call(
        functools.partial(_residual_kernel, coef=coef),
        grid=(m // tm,),
        in_specs=[blk, blk, row, row],
        out_specs=blk,
        out_shape=jax.ShapeDtypeStruct((m, d), F32),
        compiler_params=_params(1),
        name="residual",
    )(x, y, g.reshape(1, d), gate.reshape(1, d))


def _mm_kernel(a_ref, w_ref, o_ref):
    o_ref[...] = jnp.dot(a_ref[...], w_ref[...], preferred_element_type=F32).astype(o_ref.dtype)


def _mm(a, w, lead, col_off, ncols, out_dtype, tm_pref=1024, tn_pref=1024, name="mm"):
    m, k = a.shape
    tm = _pick(m, *[c for c in (1024, 512, 256, 128, 8) if c <= tm_pref])
    tn = _pick(math.gcd(col_off, ncols), *[c for c in (1024, 768, 512, 384, 256, 128) if c <= tn_pref])
    off = col_off // tn
    w_spec = pl.BlockSpec((None,) * len(lead) + (k, tn), lambda i, j: tuple(lead) + (0, off + j))
    return pl.pallas_call(
        _mm_kernel,
        grid=(m // tm, ncols // tn),
        in_specs=[pl.BlockSpec((tm, k), lambda i, j: (i, 0)), w_spec],
        out_specs=pl.BlockSpec((tm, tn), lambda i, j: (i, j)),
        out_shape=jax.ShapeDtypeStruct((m, ncols), out_dtype),
        compiler_params=_params(2),
        name=name,
    )(a, w)


def _mm_glu_kernel(a_ref, w1_ref, w2_ref, b1_ref, b2_ref, o_ref, *, swiglu):
    a = a_ref[...].astype(BF16)
    y1 = jnp.dot(a, w1_ref[...], preferred_element_type=F32) + b1_ref[...]
    y2 = jnp.dot(a, w2_ref[...], preferred_element_type=F32) + b2_ref[...]
    out = _silu(y1) * y2 if swiglu else y1 * jax.nn.sigmoid(y2)
    o_ref[...] = out.astype(o_ref.dtype)


def _mm_glu(a, w, lead, bias, swiglu, name):
    m, k = a.shape
    half = w.shape[-1] // 2
    tm = _pick(m, 1024, 512, 256, 128, 8)
    tn = _pick(half, 512, 256, 128)
    nj = half // tn
    nl = len(lead)
    w1 = pl.BlockSpec((None,) * nl + (k, tn), lambda i, j: tuple(lead) + (0, j))
    w2 = pl.BlockSpec((None,) * nl + (k, tn), lambda i, j: tuple(lead) + (0, nj + j))
    b1 = pl.BlockSpec((1, tn), lambda i, j: (0, j))
    b2 = pl.BlockSpec((1, tn), lambda i, j: (0, nj + j))
    return pl.pallas_call(
        functools.partial(_mm_glu_kernel, swiglu=swiglu),
        grid=(m // tm, nj),
        in_specs=[pl.BlockSpec((tm, k), lambda i, j: (i, 0)), w1, w2, b1, b2],
        out_specs=pl.BlockSpec((tm, tn), lambda i, j: (i, j)),
        out_shape=jax.ShapeDtypeStruct((m, half), BF16),
        compiler_params=_params(2),
        name=name,
    )(a, w, w, bias, bias)


def _merge_kernel(ya_ref, yb_ref, yc_ref, w_ref, ga_ref, gb_ref, gc_ref, o_ref):
    acc = jax.nn.sigmoid(ga_ref[...]) * jnp.dot(ya_ref[...], w_ref[0], preferred_element_type=F32)
    acc += jax.nn.sigmoid(gb_ref[...]) * jnp.dot(yb_ref[...], w_ref[1], preferred_element_type=F32)
    acc += jax.nn.sigmoid(gc_ref[...]) * jnp.dot(yc_ref[...], w_ref[2], preferred_element_type=F32)
    o_ref[...] = acc.astype(o_ref.dtype)


def _merge(ya, yb, yc, w_branch, l, gates):
    m, bw = ya.shape
    d = w_branch.shape[-1]
    tm = _pick(m, 512, 256, 128, 8)
    tn = _pick(d, 512, 256, 128)
    nj = d // tn
    y_spec = pl.BlockSpec((tm, bw), lambda i, j: (i, 0))
    g_specs = [pl.BlockSpec((tm, tn), functools.partial(lambda i, j, k: (i, k * nj + j), k=k))
               for k in range(3)]
    return pl.pallas_call(
        _merge_kernel,
        grid=(m // tm, nj),
        in_specs=[y_spec, y_spec, y_spec,
                  pl.BlockSpec((None, 3, bw, tn), lambda i, j: (l, 0, 0, j))] + g_specs,
        out_specs=pl.BlockSpec((tm, tn), lambda i, j: (i, j)),
        out_shape=jax.ShapeDtypeStruct((m, d), BF16),
        compiler_params=_params(2),
        name="branch_merge",
    )(ya, yb, yc, w_branch, gates, gates, gates)


def _na_bias_kernel(rpb_ref, o_ref, *, kw, nr, nc):
    h = pl.program_id(0)
    w = GRID_W
    q = lax.broadcasted_iota(jnp.int32, (w, 2 * w), 0)
    c2 = lax.broadcasted_iota(jnp.int32, (w, 2 * w), 1)
    second = c2 >= w
    c = jnp.where(second, c2 - w, c2)
    dc = jnp.clip(c - q, -(kw - 1), kw - 1) + (kw - 1)
    cs = jnp.clip(q - kw // 2, 0, w - kw)
    ok = jnp.logical_and(c >= cs, c < cs + kw)
    for dr in range(nr - 1):
        t = jnp.zeros((w, 2 * w), F32)
        for dd in range(nc):
            lo = rpb_ref[(h * nr + dr) * nc + dd]
            hi = rpb_ref[(h * nr + dr + 1) * nc + dd]
            t = jnp.where(dc == dd, jnp.where(second, hi, lo), t)
        o_ref[dr] = jnp.where(ok, t, NEG_INF)


def _na_bias(rpb, kw):
    heads, nr, nc = rpb.shape
    return pl.pallas_call(
        functools.partial(_na_bias_kernel, kw=kw, nr=nr, nc=nc),
        grid=(heads,),
        in_specs=[pl.BlockSpec(memory_space=pltpu.SMEM)],
        out_specs=pl.BlockSpec((None, nr - 1, GRID_W, 2 * GRID_W), lambda h: (h, 0, 0, 0)),
        out_shape=jax.ShapeDtypeStruct((heads, nr - 1, GRID_W, 2 * GRID_W), F32),
        compiler_params=_params(1),
        name="na_bias",
    )(rpb.reshape(-1))


def _na_kernel(q_ref, k_ref, v_ref, tab_ref, o_ref, *, rb, rows, kh, kh_full, scale):
    i = pl.program_id(1)
    w = GRID_W
    for rr in range(rb):
        r = i * rb + rr
        start = jnp.clip(r - kh // 2, 0, rows - kh)
        dr0 = start - r + (kh_full - 1)
        tok = pl.multiple_of(start * w, w)
        q = q_ref[rr * w:(rr + 1) * w, :]
        kwin = k_ref[pl.ds(tok, kh * w), :]
        vwin = v_ref[pl.ds(tok, kh * w), :]
        s = lax.dot_general(q, kwin, (((1,), (1,)), ((), ())), preferred_element_type=F32) * scale
        bias = jnp.concatenate([tab_ref[dr0 + 2 * mm] for mm in range(kh // 2)], axis=1)
        s = s + bias
        p = jnp.exp(s - jnp.max(s, axis=-1, keepdims=True))
        denom = jnp.sum(p, axis=-1, keepdims=True)
        o = jnp.dot(p.astype(BF16), vwin, preferred_element_type=F32)
        o_ref[rr * w:(rr + 1) * w, :] = (o / denom).astype(o_ref.dtype)


def _neighborhood_attention(qkv, rpb, kw):
    L = qkv.shape[0]
    heads, nr, _ = rpb.shape
    bw = qkv.shape[1] // 3
    hd = bw // heads
    assert hd == LANES, "neighbourhood-attention head dim must fill one lane tile"
    rows = L // GRID_W
    kh_full = (nr + 1) // 2
    kh = min(kh_full, rows)
    assert kh % 2 == 0
    tab = _na_bias(rpb, kw)
    rb = _pick(rows, 8, 4, 2, 1)
    return pl.pallas_call(
        functools.partial(_na_kernel, rb=rb, rows=rows, kh=kh, kh_full=kh_full, scale=hd ** -0.5),
        grid=(heads, rows // rb),
        in_specs=[pl.BlockSpec((rb * GRID_W, hd), lambda h, i: (i, h)),
                  pl.BlockSpec((L, hd), lambda h, i: (0, heads + h)),
                  pl.BlockSpec((L, hd), lambda h, i: (0, 2 * heads + h)),
                  pl.BlockSpec((None, nr - 1, GRID_W, 2 * GRID_W), lambda h, i: (h, 0, 0, 0))],
        out_specs=pl.BlockSpec((rb * GRID_W, hd), lambda h, i: (i, h)),
        out_shape=jax.ShapeDtypeStruct((L, bw), BF16),
        compiler_params=_params(2),
        name="neighborhood_attention",
    )(qkv, qkv, qkv, tab)


def _mm_rotary_kernel(a_ref, w_ref, cos_ref, sin_ref, o_ref, *, hd, n_q_tiles, q_scale):
    y = jnp.dot(a_ref[...], w_ref[...], preferred_element_type=F32)
    scale = jnp.where(pl.program_id(1) < n_q_tiles, q_scale, 1.0)
    cos, sin = cos_ref[...] * scale, sin_ref[...] * scale
    lane = lax.broadcasted_iota(jnp.int32, cos.shape, 1)
    first = (lane % hd) < (hd // 2)
    for t in range(y.shape[1] // LANES):
        x = y[:, t * LANES:(t + 1) * LANES]
        rot = jnp.where(first, pltpu.roll(x, LANES - hd // 2, 1), pltpu.roll(x, hd // 2, 1))
        o_ref[:, t * LANES:(t + 1) * LANES] = (x * cos + rot * sin).astype(o_ref.dtype)


def _mm_rotary(a, w, lead, col_off, ncols, hd):
    L, k = a.shape
    assert LANES % hd == 0
    half = hd // 2
    inv_freq = ROPE_THETA ** (-jnp.arange(half, dtype=F32) / half)
    ang = jnp.arange(L, dtype=F32)[:, None] * inv_freq[None, :]
    cos, sin = jnp.cos(ang), jnp.sin(ang)
    reps = LANES // hd
    cos_t = jnp.tile(jnp.concatenate([cos, cos], axis=1), (1, reps))
    sin_t = jnp.tile(jnp.concatenate([-sin, sin], axis=1), (1, reps))
    tm = _pick(L, 1024, 512, 256, 128, 8)
    tn = _pick(math.gcd(col_off, ncols // 2), 768, 512, 384, 256, 128)
    off = col_off // tn
    tab = pl.BlockSpec((tm, LANES), lambda i, j: (i, 0))
    return pl.pallas_call(
        functools.partial(_mm_rotary_kernel, hd=hd, n_q_tiles=ncols // 2 // tn,
                          q_scale=hd ** -0.5 * math.log2(math.e)),
        grid=(L // tm, ncols // tn),
        in_specs=[pl.BlockSpec((tm, k), lambda i, j: (i, 0)),
                  pl.BlockSpec((None,) * len(lead) + (k, tn), lambda i, j: tuple(lead) + (0, off + j)),
                  tab, tab],
        out_specs=pl.BlockSpec((tm, tn), lambda i, j: (i, j)),
        out_shape=jax.ShapeDtypeStruct((L, ncols), BF16),
        compiler_params=_params(2),
        name="in_proj_qk_rotary",
    )(a, w, cos_t, sin_t)


DIFF_TQ = 256
DIFF_TK = 8192
DIFF_SUB = 256
DIFF_LOOKAHEAD = 5
DIFF_ONES_ROWS = 16


def _diff_attn_kernel(q1_ref, q2_ref, k1_ref, k2_ref, v_ref, lam_ref, g_ref, o_ref,
                      qp_sc, acc1, acc2, m1, m2, *, tk, sub, hd, vd, lam_init):
    tq = q1_ref.shape[0]
    nk = v_ref.shape[0]
    par = pl.program_id(0) % (LANES // hd)
    rowi = lax.broadcasted_iota(jnp.int32, (LANES, tq), 0)
    keep = jnp.logical_and(rowi >= par * hd, rowi < (par + 1) * hd)
    for t, q_ref in enumerate((q1_ref, q2_ref)):
        qp_sc[t] = jnp.where(keep, q_ref[...].astype(F32).T, 0.0).astype(BF16)
    for acc, m in ((acc1, m1), (acc2, m2)):
        acc[...] = jnp.zeros(acc.shape, F32)
        m[...] = jnp.full(m.shape, NEG_INF, F32)

    def body(c, carry):
        off = pl.multiple_of(c * tk, tk)
        tasks = [(j, t) for j in range(tk // sub) for t in range(2)]
        k_refs, accs, ms = (k1_ref, k2_ref), (acc1, acc2), (m1, m2)

        def scores(j, t):
            ks = k_refs[t][pl.ds(off + j * sub, sub), :]
            return jnp.dot(ks, qp_sc[t], preferred_element_type=F32)

        pending = [scores(*task) for task in tasks[:DIFF_LOOKAHEAD]]
        for idx, (j, t) in enumerate(tasks):
            s = pending.pop(0)
            if idx + DIFF_LOOKAHEAD < len(tasks):
                pending.append(scores(*tasks[idx + DIFF_LOOKAHEAD]))
            m_old = ms[t][...]
            m_new = jnp.maximum(m_old, jnp.max(s, axis=0, keepdims=True))
            p = jnp.exp2(s - m_new).astype(BF16)
            pv = jnp.dot(v_ref[c, :, j * sub:(j + 1) * sub], p, preferred_element_type=F32)
            accs[t][...] = jnp.exp2(m_old - m_new) * accs[t][...] + pv
            ms[t][...] = m_new
        return carry

    lax.fori_loop(0, nk, body, 0)
    lp = lam_ref[...]
    lam = (jnp.exp(jnp.sum(lp[0:1] * lp[1:2], axis=-1, keepdims=True))
           - jnp.exp(jnp.sum(lp[2:3] * lp[3:4], axis=-1, keepdims=True)) + lam_init)
    a1, a2 = acc1[...], acc2[...]
    o = a1[:vd] / a1[vd:vd + 1] - lam * (a2[:vd] / a2[vd:vd + 1])
    o = o * lax.rsqrt(jnp.mean(o * o, axis=0, keepdims=True) + RMS_EPS) * g_ref[...]
    o_ref[...] = (o * (1.0 - lam_init)).astype(o_ref.dtype)


def _diff_attention(qk_rot, v, diff_lam, subln_g, lam_init):
    L = qk_rot.shape[0]
    hd = diff_lam.shape[-1]
    heads = qk_rot.shape[1] // (4 * hd)
    vd = v.shape[1] // heads
    assert vd == LANES and LANES % hd == 0 and (heads * hd) % LANES == 0
    hp = LANES // hd
    nb = heads * hd // LANES
    tq = _pick(L, DIFF_TQ, 128)
    tk = _pick(L, DIFF_TK, 4096, 2048, 1024, 512, 256, 128)
    nk = L // tk
    vt = jnp.transpose(v.reshape(nk, tk, heads, vd), (2, 0, 3, 1))
    vt = jnp.concatenate([vt, jnp.ones((heads, nk, DIFF_ONES_ROWS, tk), BF16)], axis=2)
    acc = pltpu.VMEM((vd + DIFF_ONES_ROWS, tq), F32)
    stat = pltpu.VMEM((1, tq), F32)
    out_t = pl.pallas_call(
        functools.partial(_diff_attn_kernel, tk=tk, sub=_pick(tk, DIFF_SUB), hd=hd, vd=vd, lam_init=lam_init),
        grid=(heads, L // tq),
        in_specs=[pl.BlockSpec((tq, LANES), lambda h, i: (i, h // hp)),
                  pl.BlockSpec((tq, LANES), lambda h, i: (i, nb + h // hp)),
                  pl.BlockSpec((L, LANES), lambda h, i: (0, 2 * nb + h // hp)),
                  pl.BlockSpec((L, LANES), lambda h, i: (0, 3 * nb + h // hp)),
                  pl.BlockSpec((None, nk, vd + DIFF_ONES_ROWS, tk), lambda h, i: (h, 0, 0, 0)),
                  pl.BlockSpec(diff_lam.shape, lambda h, i: (0, 0)),
                  pl.BlockSpec((vd, 1), lambda h, i: (0, 0))],
        out_specs=pl.BlockSpec((None, vd, tq), lambda h, i: (h, 0, i)),
        out_shape=jax.ShapeDtypeStruct((heads, vd, L), BF16),
        scratch_shapes=[pltpu.VMEM((2, LANES, tq), BF16), acc, acc, stat, stat],
        compiler_params=_params(2),
        name="diff_attention",
    )(qk_rot, qk_rot, qk_rot, qk_rot, vt, diff_lam, subln_g.reshape(vd, 1))
    return jnp.transpose(out_t, (2, 0, 1)).reshape(L, heads * vd)


def _s5_tables(lam_re, lam_im, log_dt, b_re, b_im, c_re, c_im, d_skip):
    T = S5_CHUNK
    _, G, P, HG = b_re.shape
    dt = jnp.exp(log_dt.astype(F32))[..., None]
    lr, li = lam_re.astype(F32), lam_im.astype(F32)
    j = jnp.arange(T + 1, dtype=F32)[:, None, None, None]
    mag = jnp.exp(lr * dt * j)
    pw_re, pw_im = mag * jnp.cos(li * dt * j), mag * jnp.sin(li * dt * j)
    den = lr * lr + li * li
    nr_, ni_ = pw_re[1] - 1.0, pw_im[1]
    f_re, f_im = (nr_ * lr + ni_ * li) / den, (ni_ * lr - nr_ * li) / den
    bb_re = f_re[..., None] * b_re - f_im[..., None] * b_im
    bb_im = f_re[..., None] * b_im + f_im[..., None] * b_re
    cr, ci = c_re.astype(F32)[None], c_im.astype(F32)[None]
    cp_re = cr * pw_re[:, :, :, None, :] - ci * pw_im[:, :, :, None, :]
    cp_im = cr * pw_im[:, :, :, None, :] + ci * pw_re[:, :, :, None, :]
    hp = lax.Precision.HIGHEST
    kj = (jnp.einsum('jdghp,dgpk->jdghk', cp_re[:T], bb_re, precision=hp)
          - jnp.einsum('jdghp,dgpk->jdghk', cp_im[:T], bb_im, precision=hp))
    sig = jnp.arange(T)[:, None]
    tau = jnp.arange(T)[None, :]
    lag_f = jnp.clip(tau - sig, 0, T - 1)
    lag_b = jnp.clip(sig - tau, 0, T - 1)
    kf = jnp.where((tau >= sig)[:, :, None, None, None], kj[lag_f, 0], 0.0)
    kb = jnp.where((sig >= tau)[:, :, None, None, None], kj[lag_b, 1], 0.0)
    kin = jnp.transpose(kf + kb, (2, 0, 4, 1, 3)).reshape(G, T * HG, T * HG)
    def bend_dir(d, order):
        pr, pi = pw_re[order, d], pw_im[order, d]
        re = pr[..., None] * bb_re[d][None] - pi[..., None] * bb_im[d][None]
        im = pr[..., None] * bb_im[d][None] + pi[..., None] * bb_re[d][None]
        to_rows = lambda t: jnp.transpose(t, (1, 0, 3, 2)).reshape(G, T * HG, P)
        return to_rows(re), to_rows(im)
    bf_re, bf_im = bend_dir(0, jnp.arange(T - 1, -1, -1))
    bb_re_, bb_im_ = bend_dir(1, jnp.arange(T))
    bend = jnp.stack([bf_re, bf_im, bb_re_, bb_im_], axis=2)
    def wout_dir(d, order):
        to_cols = lambda t: jnp.transpose(t, (1, 3, 0, 2)).reshape(G, P, T * HG)
        return to_cols(cp_re[order, d]), to_cols(-cp_im[order, d])
    wf_re, wf_im = wout_dir(0, jnp.arange(1, T + 1))
    wb_re, wb_im = wout_dir(1, jnp.arange(T, 0, -1))
    wout = jnp.stack([wf_re, wf_im, wb_re, wb_im], axis=1)
    eye2 = jnp.eye(2, dtype=F32)
    G2 = G // 2
    kin2 = jnp.einsum('karc,ab->karbc', kin.reshape(G2, 2, T * HG, T * HG), eye2)
    kin2 = kin2.reshape(G2, 2 * T * HG, 2 * T * HG)
    bend2 = jnp.einsum('karqp,ab->karqbp', bend.reshape(G2, 2, T * HG, 4, P), eye2)
    bend2 = bend2.reshape(G2, 2 * T * HG, 4 * 2 * P)
    wout2 = jnp.einsum('kaqpc,ab->kqapbc', wout.reshape(G2, 2, 4, P, T * HG), eye2)
    wout2 = wout2.reshape(G2, 4 * 2 * P, 2 * T * HG)
    a = jnp.stack([pw_re[T, 0], pw_im[T, 0], pw_re[T, 1], pw_im[T, 1]], axis=1)
    a2 = jnp.transpose(a.reshape(G2, 2, 4, P), (0, 2, 1, 3)).reshape(G2, 4, 2 * P)
    dsk = jnp.tile(d_skip.astype(F32)[:, None, :], (1, T, 1)).reshape(G2, 1, 2 * T * HG)
    return kin2.astype(BF16), bend2.astype(BF16), wout2.astype(BF16), a2, dsk


def _s5_kernel(u_ref, kin_ref, bend_ref, wout_ref, a_ref, d_ref, o_ref, z_sc, s_sc):
    nc = u_ref.shape[0]
    sw = a_ref.shape[1]
    u = u_ref[...]
    ub = u.astype(BF16)
    z_sc[...] = jnp.dot(ub, bend_ref[...], preferred_element_type=F32)
    sub = 8
    nt = nc // sub
    afr, afi, abr, abi = (jnp.broadcast_to(a_ref[t:t + 1, :], (sub, sw)) for t in range(4))
    row = lax.broadcasted_iota(jnp.int32, (sub, sw), 0)

    def body(i, carry):
        xfr, xfi, xbr, xbi = carry
        rf = pl.multiple_of(i * sub, sub)
        rb = pl.multiple_of((nt - 1 - i) * sub, sub)
        zf = z_sc[pl.ds(rf, sub), 0:2 * sw]
        zb = z_sc[pl.ds(rb, sub), 2 * sw:4 * sw]
        sfr = sfi = sbr = sbi = jnp.zeros((sub, sw), F32)
        for jf in range(sub):
            jb = sub - 1 - jf
            sfr = jnp.where(row == jf, xfr, sfr)
            sfi = jnp.where(row == jf, xfi, sfi)
            sbr = jnp.where(row == jb, xbr, sbr)
            sbi = jnp.where(row == jb, xbi, sbi)
            zfj = jnp.broadcast_to(zf[jf:jf + 1, :], (sub, 2 * sw))
            zbj = jnp.broadcast_to(zb[jb:jb + 1, :], (sub, 2 * sw))
            xfr, xfi = (afr * xfr - afi * xfi + zfj[:, 0:sw], afr * xfi + afi * xfr + zfj[:, sw:2 * sw])
            xbr, xbi = (abr * xbr - abi * xbi + zbj[:, 0:sw], abr * xbi + abi * xbr + zbj[:, sw:2 * sw])
        s_sc[pl.ds(rf, sub), 0:sw] = sfr
        s_sc[pl.ds(rf, sub), sw:2 * sw] = sfi
        s_sc[pl.ds(rb, sub), 2 * sw:3 * sw] = sbr
        s_sc[pl.ds(rb, sub), 3 * sw:4 * sw] = sbi
        return xfr, xfi, xbr, xbi

    zero = jnp.zeros((sub, sw), F32)
    lax.fori_loop(0, nt, body, (zero, zero, zero, zero))
    y = jnp.dot(ub, kin_ref[...], preferred_element_type=F32)
    y += jnp.dot(s_sc[...].astype(BF16), wout_ref[...], preferred_element_type=F32)
    y += d_ref[...] * u
    o_ref[...] = _gelu_tanh(y).astype(o_ref.dtype)


def _s5_bidirectional(u, tables):
    kin, bend, wout, a, dsk = tables
    L, bw = u.shape
    T = S5_CHUNK
    g2 = kin.shape[0]
    cw = kin.shape[1]
    hg = cw // (2 * T)
    nc = L // T
    assert L % T == 0 and nc % 8 == 0 and g2 * 2 * hg == bw
    up = jnp.transpose(u.reshape(nc, T, g2, 2, hg), (0, 2, 3, 1, 4)).reshape(nc, g2 * cw)
    sw4 = bend.shape[2]
    z = pl.pallas_call(
        _s5_kernel,
        grid=(g2,),
        in_specs=[pl.BlockSpec((nc, cw), lambda g: (0, g)),
                  pl.BlockSpec((None, cw, cw), lambda g: (g, 0, 0)),
                  pl.BlockSpec((None, cw, sw4), lambda g: (g, 0, 0)),
                  pl.BlockSpec((None, sw4, cw), lambda g: (g, 0, 0)),
                  pl.BlockSpec((None, 4, sw4 // 4), lambda g: (g, 0, 0)),
                  pl.BlockSpec((None, 1, cw), lambda g: (g, 0, 0))],
        out_specs=pl.BlockSpec((nc, cw), lambda g: (0, g)),
        out_shape=jax.ShapeDtypeStruct((nc, g2 * cw), F32),
        scratch_shapes=[pltpu.VMEM((nc, sw4), F32), pltpu.VMEM((nc, sw4), F32)],
        compiler_params=_params(1),
        name="s5_scan",
    )(up, kin, bend, wout, a, dsk)
    return jnp.transpose(z.reshape(nc, g2, 2, T, hg), (0, 3, 1, 2, 4)).reshape(L, bw)


def _encoder_trunk(x, mod, wts, prm):
    L, d = x.shape
    depth = wts["ffn_w_in"].shape[0]
    bw = wts["w_branch"].shape[2]
    hd = prm["diff_lam"].shape[-1]
    kw = (prm["na_rpb"].shape[-1] + 1) // 2
    qk_cols = wts["w_in"].shape[-1] - 5 * bw - 3 * d
    zero_bias = jnp.zeros((1, wts["ffn_w_in"].shape[-1]), F32)

    def mod_rows(l, sub):
        return tuple(mod[l, (3 * sub + t) * d:(3 * sub + t + 1) * d] for t in range(3))

    def ffn(x, l, which, sub):
        shift, scale, gate = mod_rows(l, sub)
        h = _modulate(x, prm["norm_pre"][l, sub], scale, shift)
        act = _mm_glu(h, wts["ffn_w_in"], (l, which), zero_bias, True, "ffn_up")
        y = _mm(act, wts["ffn_w_out"], (l, which), 0, d, F32, tm_pref=512, tn_pref=256, name="ffn_down")
        return _residual(x, y, prm["norm_post"][l, sub], gate, 0.5)

    for l in range(depth):
        lam_init = 0.8 - 0.6 * math.exp(-0.3 * l)
        x = ffn(x, l, 0, 0)
        shift, scale, gate = mod_rows(l, 1)
        h = _modulate(x, prm["norm_pre"][l, 1], scale, shift)
        w_in = wts["w_in"]
        qkv_a = _mm(h, w_in, (l,), 0, 3 * bw, BF16, name="in_proj_na")
        u = _mm(h, w_in, (l,), 3 * bw, bw, F32, name="in_proj_s5")
        qk_rot = _mm_rotary(h, w_in, (l,), 4 * bw, qk_cols, hd)
        vc = _mm(h, w_in, (l,), 4 * bw + qk_cols, bw, BF16, name="in_proj_v")
        gates = _mm(h, w_in, (l,), 5 * bw + qk_cols, 3 * d, F32, name="in_proj_gates")
        y_a = _neighborhood_attention(qkv_a, prm["na_rpb"][l], kw)
        z = _s5_bidirectional(u, prm["s5_tables"][l])
        y_b = _mm_glu(z, wts["s5_w_glu"], (l,), prm["s5_b_glu"][l].reshape(1, -1), False, "s5_glu")
        y_c = _diff_attention(qk_rot, vc, prm["diff_lam"][l], prm["diff_subln"][l], lam_init)
        merged = _merge(y_a, y_b, y_c, wts["w_branch"], l, gates)
        m = _mm(merged, wts["w_out"], (l,), 0, d, F32, name="out_proj")
        x = _residual(x, m, prm["norm_post"][l, 1], gate, 1.0)
        x = ffn(x, l, 1, 2)
    return x


def kernel(x_prompt, x_sample, c_prompt, c_sample, ada_w, ada_b, norm_pre, norm_post, ffn_w_in, ffn_w_out,
           w_in, w_branch, w_out, na_rpb, s5_lam_re, s5_lam_im, s5_log_dt, s5_b_re, s5_b_im, s5_c_re,
           s5_c_im, s5_d, s5_w_glu, s5_b_glu, diff_lam, diff_subln):
    depth = ada_w.shape[0]
    wts = {"ffn_w_in": ffn_w_in.astype(BF16), "ffn_w_out": ffn_w_out.astype(BF16),
           "w_in": w_in.astype(BF16), "w_branch": w_branch.astype(BF16),
           "w_out": w_out.astype(BF16), "s5_w_glu": s5_w_glu.astype(BF16)}
    prm = {"norm_pre": norm_pre, "norm_post": norm_post, "na_rpb": na_rpb, "s5_b_glu": s5_b_glu,
           "diff_lam": diff_lam, "diff_subln": diff_subln,
           "s5_tables": [_s5_tables(s5_lam_re[l], s5_lam_im[l], s5_log_dt[l], s5_b_re[l], s5_b_im[l],
                                    s5_c_re[l], s5_c_im[l], s5_d[l]) for l in range(depth)]}
    c_all = jnp.concatenate([c_prompt, c_sample], axis=0)
    mod = _ada_mod(c_all, ada_w, ada_b)
    outs = []
    row = 0
    for x in (x_prompt, x_sample):
        ys = []
        for b in range(x.shape[0]):
            ys.append(_encoder_trunk(x[b], mod[:, row], wts, prm))
            row += 1
        outs.append(ys[0][None] if len(ys) == 1 else jnp.stack(ys, axis=0))
    return tuple(outs)
```

```python
import functools
import math

import jax
import jax.numpy as jnp
from jax import lax
from jax.experimental import pallas as pl
from jax.experimental.pallas import tpu as pltpu

F32 = jnp.float32
BF16 = jnp.bfloat16

GRID_W = 64
ROPE_THETA = 10000.0
RMS_EPS = 1e-6
NEG_INF = -1e30
LANES = 128
S5_CHUNK = 16
VMEM_LIMIT_BYTES = 56 * 1024 * 1024


def _params(ndims):
    return pltpu.CompilerParams(dimension_semantics=("arbitrary",) * ndims,
                                vmem_limit_bytes=VMEM_LIMIT_BYTES)


def _pick(n, *cands):
    for c in cands:
        if n % c == 0:
            return c
    return n


def _silu(x):
    return x * jax.nn.sigmoid(x)


def _gelu_tanh(x):
    return 0.5 * x * (1.0 + jnp.tanh(math.sqrt(2.0 / math.pi) * (x + 0.044715 * (x * x * x))))


def _ada_kernel(c_ref, w_ref, b_ref, o_ref, *, kc):
    d, nb = c_ref.shape
    tn = w_ref.shape[1]

    def body(k, accs):
        off = pl.multiple_of(k * kc, kc)
        wk = w_ref[pl.ds(off, kc), :]
        ck = _silu(c_ref[pl.ds(off, kc), :])
        return tuple(acc + jnp.sum(wk * ck[:, b:b + 1], axis=0, keepdims=True)
                     for b, acc in enumerate(accs))

    accs = lax.fori_loop(0, d // kc, body, tuple(jnp.zeros((1, tn), F32) for _ in range(nb)))
    for b in range(nb):
        o_ref[b:b + 1, :] = accs[b] + b_ref[...]


def _ada_mod(c_all, ada_w, ada_b):
    nb, d = c_all.shape
    depth, _, n = ada_w.shape
    tn = _pick(n, 512, 256, 128)
    kc = _pick(d, 256, 128, 8)
    return pl.pallas_call(
        functools.partial(_ada_kernel, kc=kc),
        grid=(depth, n // tn),
        in_specs=[pl.BlockSpec((d, nb), lambda l, j: (0, 0)),
                  pl.BlockSpec((None, d, tn), lambda l, j: (l, 0, j)),
                  pl.BlockSpec((None, 1, tn), lambda l, j: (l, 0, j))],
        out_specs=pl.BlockSpec((None, nb, tn), lambda l, j: (l, 0, j)),
        out_shape=jax.ShapeDtypeStruct((depth, nb, n), F32),
        compiler_params=_params(2),
        name="ada_mod",
    )(c_all.T, ada_w, ada_b.reshape(depth, 1, n))


def _modulate_kernel(x_ref, g_ref, sc_ref, sh_ref, o_ref):
    x = x_ref[...]
    y = x * lax.rsqrt(jnp.mean(x * x, axis=-1, keepdims=True) + RMS_EPS) * g_ref[...]
    o_ref[...] = (y * (1.0 + sc_ref[...]) + sh_ref[...]).astype(o_ref.dtype)


def _modulate(x, g, scale, shift):
    m, d = x.shape
    tm = _pick(m, 256, 128, 8)
    row = pl.BlockSpec((1, d), lambda i: (0, 0))
    return pl.pallas_call(
        _modulate_kernel,
        grid=(m // tm,),
        in_specs=[pl.BlockSpec((tm, d), lambda i: (i, 0)), row, row, row],
        out_specs=pl.BlockSpec((tm, d), lambda i: (i, 0)),
        out_shape=jax.ShapeDtypeStruct((m, d), BF16),
        compiler_params=_params(1),
        name="modulate",
    )(x, g.reshape(1, d), scale.reshape(1, d), shift.reshape(1, d))


def _residual_kernel(x_ref, y_ref, g_ref, gate_ref, o_ref, *, coef):
    y = y_ref[...]
    n = y * lax.rsqrt(jnp.mean(y * y, axis=-1, keepdims=True) + RMS_EPS) * g_ref[...]
    o_ref[...] = x_ref[...] + (coef * gate_ref[...]) * n


def _residual(x, y, g, gate, coef):
    m, d = x.shape
    tm = _pick(m, 256, 128, 8)
    row = pl.BlockSpec((1, d), lambda i: (0, 0))
    blk = pl.BlockSpec((tm, d), lambda i: (i, 0))
    return pl.pallas_call(
        functools.partial(_residual_kernel, coef=coef),
        grid=(m // tm,),
        in_specs=[blk, blk, row, row],
        out_specs=blk,
        out_shape=jax.ShapeDtypeStruct((m, d), F32),
        compiler_params=_params(1),
        name="residual",
    )(x, y, g.reshape(1, d), gate.reshape(1, d))


def _mm_kernel(a_ref, w_ref, o_ref):
    o_ref[...] = jnp.dot(a_ref[...], w_ref[...], preferred_element_type=F32).astype(o_ref.dtype)


def _mm(a, w, lead, col_off, ncols, out_dtype, tm_pref=1024, tn_pref=1024, name="mm"):
    m, k = a.shape
    tm = _pick(m, *[c for c in (1024, 512, 256, 128, 8) if c <= tm_pref])
    tn = _pick(math.gcd(col_off, ncols), *[c for c in (1024, 768, 512, 384, 256, 128) if c <= tn_pref])
    off = col_off // tn
    w_spec = pl.BlockSpec((None,) * len(lead) + (k, tn), lambda i, j: tuple(lead) + (0, off + j))
    return pl.pallas_call(
        _mm_kernel,
        grid=(m // tm, ncols // tn),
        in_specs=[pl.BlockSpec((tm, k), lambda i, j: (i, 0)), w_spec],
        out_specs=pl.BlockSpec((tm, tn), lambda i, j: (i, j)),
        out_shape=jax.ShapeDtypeStruct((m, ncols), out_dtype),
        compiler_params=_params(2),
        name=name,
    )(a, w)


def _mm_glu_kernel(a_ref, w1_ref, w2_ref, b1_ref, b2_ref, o_ref, *, swiglu):
    a = a_ref[...].astype(BF16)
    y1 = jnp.dot(a, w1_ref[...], preferred_element_type=F32) + b1_ref[...]
    y2 = jnp.dot(a, w2_ref[...], preferred_element_type=F32) + b2_ref[...]
    out = _silu(y1) * y2 if swiglu else y1 * jax.nn.sigmoid(y2)
    o_ref[...] = out.astype(o_ref.dtype)


def _mm_glu(a, w, lead, bias, swiglu, name):
    m, k = a.shape
    half = w.shape[-1] // 2
    tm = _pick(m, 1024, 512, 256, 128, 8)
    tn = _pick(half, 512, 256, 128)
    nj = half // tn
    nl = len(lead)
    w1 = pl.BlockSpec((None,) * nl + (k, tn), lambda i, j: tuple(lead) + (0, j))
    w2 = pl.BlockSpec((None,) * nl + (k, tn), lambda i, j: tuple(lead) + (0, nj + j))
    b1 = pl.BlockSpec((1, tn), lambda i, j: (0, j))
    b2 = pl.BlockSpec((1, tn), lambda i, j: (0, nj + j))
    return pl.pallas_call(
        functools.partial(_mm_glu_kernel, swiglu=swiglu),
        grid=(m // tm, nj),
        in_specs=[pl.BlockSpec((tm, k), lambda i, j: (i, 0)), w1, w2, b1, b2],
        out_specs=pl.BlockSpec((tm, tn), lambda i, j: (i, j)),
        out_shape=jax.ShapeDtypeStruct((m, half), BF16),
        compiler_params=_params(2),
        name=name,
    )(a, w, w, bias, bias)


def _merge_kernel(ya_ref, yb_ref, yc_ref, w_ref, ga_ref, gb_ref, gc_ref, o_ref):
    acc = jax.nn.sigmoid(ga_ref[...]) * jnp.dot(ya_ref[...], w_ref[0], preferred_element_type=F32)
    acc += jax.nn.sigmoid(gb_ref[...]) * jnp.dot(yb_ref[...], w_ref[1], preferred_element_type=F32)
    acc += jax.nn.sigmoid(gc_ref[...]) * jnp.dot(yc_ref[...], w_ref[2], preferred_element_type=F32)
    o_ref[...] = acc.astype(o_ref.dtype)


def _merge(ya, yb, yc, w_branch, l, gates):
    m, bw = ya.shape
    d = w_branch.shape[-1]
    tm = _pick(m, 512, 256, 128, 8)
    tn = _pick(d, 512, 256, 128)
    nj = d // tn
    y_spec = pl.BlockSpec((tm, bw), lambda i, j: (i, 0))
    g_specs = [pl.BlockSpec((tm, tn), functools.partial(lambda i, j, k: (i, k * nj + j), k=k))
               for k in range(3)]
    return pl.pallas_call(
        _merge_kernel,
        grid=(m // tm, nj),
        in_specs=[y_spec, y_spec, y_spec,
                  pl.BlockSpec((None, 3, bw, tn), lambda i, j: (l, 0, 0, j))] + g_specs,
        out_specs=pl.BlockSpec((tm, tn), lambda i, j: (i, j)),
        out_shape=jax.ShapeDtypeStruct((m, d), BF16),
        compiler_params=_params(2),
        name="branch_merge",
    )(ya, yb, yc, w_branch, gates, gates, gates)


def _na_bias_kernel(rpb_ref, o_ref, *, kw, nr, nc):
    h = pl.program_id(0)
    w = GRID_W
    q = lax.broadcasted_iota(jnp.int32, (w, 2 * w), 0)
    c2 = lax.broadcasted_iota(jnp.int32, (w, 2 * w), 1)
    second = c2 >= w
    c = jnp.where(second, c2 - w, c2)
    dc = jnp.clip(c - q, -(kw - 1), kw - 1) + (kw - 1)
    cs = jnp.clip(q - kw // 2, 0, w - kw)
    ok = jnp.logical_and(c >= cs, c < cs + kw)
    for dr in range(nr - 1):
        t = jnp.zeros((w, 2 * w), F32)
        for dd in range(nc):
            lo = rpb_ref[(h * nr + dr) * nc + dd]
            hi = rpb_ref[(h * nr + dr + 1) * nc + dd]
            t = jnp.where(dc == dd, jnp.where(second, hi, lo), t)
        o_ref[dr] = jnp.where(ok, t, NEG_INF)


def _na_bias(rpb, kw):
    heads, nr, nc = rpb.shape
    return pl.pallas_call(
        functools.partial(_na_bias_kernel, kw=kw, nr=nr, nc=nc),
        grid=(heads,),
        in_specs=[pl.BlockSpec(memory_space=pltpu.SMEM)],
        out_specs=pl.BlockSpec((None, nr - 1, GRID_W, 2 * GRID_W), lambda h: (h, 0, 0, 0)),
        out_shape=jax.ShapeDtypeStruct((heads, nr - 1, GRID_W, 2 * GRID_W), F32),
        compiler_params=_params(1),
        name="na_bias",
    )(rpb.reshape(-1))


def _na_kernel(q_ref, k_ref, v_ref, tab_ref, o_ref, *, rb, rows, kh, kh_full, scale):
    i = pl.program_id(1)
    w = GRID_W
    scores, toks = [], []
    for rr in range(rb):
        r = i * rb + rr
        start = jnp.clip(r - kh // 2, 0, rows - kh)
        dr0 = start - r + (kh_full - 1)
        tok = pl.multiple_of(start * w, w)
        q = q_ref[rr * w:(rr + 1) * w, :]
        kwin = k_ref[pl.ds(tok, kh * w), :]
        s = lax.dot_general(q, kwin, (((1,), (1,)), ((), ())), preferred_element_type=F32) * scale
        bias = jnp.concatenate([tab_ref[dr0 + 2 * mm] for mm in range(kh // 2)], axis=1)
        scores.append(s + bias)
        toks.append(tok)
    for rr in range(rb):
        s = scores[rr]
        p = jnp.exp(s - jnp.max(s, axis=-1, keepdims=True))
        denom = jnp.sum(p, axis=-1, keepdims=True)
        o = jnp.dot(p.astype(BF16), v_ref[pl.ds(toks[rr], kh * w), :], preferred_element_type=F32)
        o_ref[rr * w:(rr + 1) * w, :] = (o / denom).astype(o_ref.dtype)


def _neighborhood_attention(qkv, rpb, kw):
    L = qkv.shape[0]
    heads, nr, _ = rpb.shape
    bw = qkv.shape[1] // 3
    hd = bw // heads
    assert hd == LANES, "neighbourhood-attention head dim must fill one lane tile"
    rows = L // GRID_W
    kh_full = (nr + 1) // 2
    kh = min(kh_full, rows)
    assert kh % 2 == 0
    tab = _na_bias(rpb, kw)
    rb = _pick(rows, 8, 4, 2, 1)
    return pl.pallas_call(
        functools.partial(_na_kernel, rb=rb, rows=rows, kh=kh, kh_full=kh_full, scale=hd ** -0.5),
        grid=(heads, rows // rb),
        in_specs=[pl.BlockSpec((rb * GRID_W, hd), lambda h, i: (i, h)),
                  pl.BlockSpec((L, hd), lambda h, i: (0, heads + h)),
                  pl.BlockSpec((L, hd), lambda h, i: (0, 2 * heads + h)),
                  pl.BlockSpec((None, nr - 1, GRID_W, 2 * GRID_W), lambda h, i: (h, 0, 0, 0))],
        out_specs=pl.BlockSpec((rb * GRID_W, hd), lambda h, i: (i, h)),
        out_shape=jax.ShapeDtypeStruct((L, bw), BF16),
        compiler_params=_params(2),
        name="neighborhood_attention",
    )(qkv, qkv, qkv, tab)


def _mm_rotary_kernel(a_ref, w_ref, cos_ref, sin_ref, o_ref, *, hd, n_q_tiles, q_scale):
    y = jnp.dot(a_ref[...], w_ref[...], preferred_element_type=F32)
    scale = jnp.where(pl.program_id(1) < n_q_tiles, q_scale, 1.0)
    cos, sin = cos_ref[...] * scale, sin_ref[...] * scale
    lane = lax.broadcasted_iota(jnp.int32, cos.shape, 1)
    first = (lane % hd) < (hd // 2)
    for t in range(y.shape[1] // LANES):
        x = y[:, t * LANES:(t + 1) * LANES]
        rot = jnp.where(first, pltpu.roll(x, LANES - hd // 2, 1), pltpu.roll(x, hd // 2, 1))
        o_ref[:, t * LANES:(t + 1) * LANES] = (x * cos + rot * sin).astype(o_ref.dtype)


def _mm_rotary(a, w, lead, col_off, ncols, hd):
    L, k = a.shape
    assert LANES % hd == 0
    half = hd // 2
    inv_freq = ROPE_THETA ** (-jnp.arange(half, dtype=F32) / half)
    ang = jnp.arange(L, dtype=F32)[:, None] * inv_freq[None, :]
    cos, sin = jnp.cos(ang), jnp.sin(ang)
    reps = LANES // hd
    cos_t = jnp.tile(jnp.concatenate([cos, cos], axis=1), (1, reps))
    sin_t = jnp.tile(jnp.concatenate([-sin, sin], axis=1), (1, reps))
    tm = _pick(L, 1024, 512, 256, 128, 8)
    tn = _pick(math.gcd(col_off, ncols // 2), 768, 512, 384, 256, 128)
    off = col_off // tn
    tab = pl.BlockSpec((tm, LANES), lambda i, j: (i, 0))
    return pl.pallas_call(
        functools.partial(_mm_rotary_kernel, hd=hd, n_q_tiles=ncols // 2 // tn,
                          q_scale=hd ** -0.5 * math.log2(math.e)),
        grid=(L // tm, ncols // tn),
        in_specs=[pl.BlockSpec((tm, k), lambda i, j: (i, 0)),
                  pl.BlockSpec((None,) * len(lead) + (k, tn), lambda i, j: tuple(lead) + (0, off + j)),
                  tab, tab],
        out_specs=pl.BlockSpec((tm, tn), lambda i, j: (i, j)),
        out_shape=jax.ShapeDtypeStruct((L, ncols), BF16),
        compiler_params=_params(2),
        name="in_proj_qk_rotary",
    )(a, w, cos_t, sin_t)


DIFF_TQ = 256
DIFF_TK = 8192
DIFF_SUB = 256
DIFF_LOOKAHEAD = 5
DIFF_ONES_ROWS = 16


def _diff_attn_kernel(q1_ref, q2_ref, k1_ref, k2_ref, v_ref, lam_ref, g_ref, o_ref,
                      qp_sc, acc1, acc2, m1, m2, *, tk, sub, hd, vd, lam_init):
    tq = q1_ref.shape[0]
    nk = v_ref.shape[0]
    par = pl.program_id(0) % (LANES // hd)
    rowi = lax.broadcasted_iota(jnp.int32, (LANES, tq), 0)
    keep = jnp.logical_and(rowi >= par * hd, rowi < (par + 1) * hd)
    for t, q_ref in enumerate((q1_ref, q2_ref)):
        qp_sc[t] = jnp.where(keep, q_ref[...].astype(F32).T, 0.0).astype(BF16)
    for acc, m in ((acc1, m1), (acc2, m2)):
        acc[...] = jnp.zeros(acc.shape, F32)
        m[...] = jnp.full(m.shape, NEG_INF, F32)

    def body(c, carry):
        off = pl.multiple_of(c * tk, tk)
        tasks = [(j, t) for j in range(tk // sub) for t in range(2)]
        k_refs, accs, ms = (k1_ref, k2_ref), (acc1, acc2), (m1, m2)

        def scores(j, t):
            ks = k_refs[t][pl.ds(off + j * sub, sub), :]
            return jnp.dot(ks, qp_sc[t], preferred_element_type=F32)

        pending = [scores(*task) for task in tasks[:DIFF_LOOKAHEAD]]
        for idx, (j, t) in enumerate(tasks):
            s = pending.pop(0)
            if idx + DIFF_LOOKAHEAD < len(tasks):
                pending.append(scores(*tasks[idx + DIFF_LOOKAHEAD]))
            m_old = ms[t][...]
            m_new = jnp.maximum(m_old, jnp.max(s, axis=0, keepdims=True))
            p = jnp.exp2(s - m_new).astype(BF16)
            pv = jnp.dot(v_ref[c, :, j * sub:(j + 1) * sub], p, preferred_element_type=F32)
            accs[t][...] = jnp.exp2(m_old - m_new) * accs[t][...] + pv
            ms[t][...] = m_new
        return carry

    lax.fori_loop(0, nk, body, 0)
    lp = lam_ref[...]
    lam = (jnp.exp(jnp.sum(lp[0:1] * lp[1:2], axis=-1, keepdims=True))
           - jnp.exp(jnp.sum(lp[2:3] * lp[3:4], axis=-1, keepdims=True)) + lam_init)
    a1, a2 = acc1[...], acc2[...]
    o = a1[:vd] / a1[vd:vd + 1] - lam * (a2[:vd] / a2[vd:vd + 1])
    o = o * lax.rsqrt(jnp.mean(o * o, axis=0, keepdims=True) + RMS_EPS) * g_ref[...]
    o_ref[...] = (o * (1.0 - lam_init)).astype(o_ref.dtype)


def _diff_attention(qk_rot, v, diff_lam, subln_g, lam_init):
    L = qk_rot.shape[0]
    hd = diff_lam.shape[-1]
    heads = qk_rot.shape[1] // (4 * hd)
    vd = v.shape[1] // heads
    assert vd == LANES and LANES % hd == 0 and (heads * hd) % LANES == 0
    hp = LANES // hd
    nb = heads * hd // LANES
    tq = _pick(L, DIFF_TQ, 128)
    tk = _pick(L, DIFF_TK, 4096, 2048, 1024, 512, 256, 128)
    nk = L // tk
    vt = jnp.transpose(v.reshape(nk, tk, heads, vd), (2, 0, 3, 1))
    vt = jnp.concatenate([vt, jnp.ones((heads, nk, DIFF_ONES_ROWS, tk), BF16)], axis=2)
    acc = pltpu.VMEM((vd + DIFF_ONES_ROWS, tq), F32)
    stat = pltpu.VMEM((1, tq), F32)
    out_t = pl.pallas_call(
        functools.partial(_diff_attn_kernel, tk=tk, sub=_pick(tk, DIFF_SUB), hd=hd, vd=vd, lam_init=lam_init),
        grid=(heads, L // tq),
        in_specs=[pl.BlockSpec((tq, LANES), lambda h, i: (i, h // hp)),
                  pl.BlockSpec((tq, LANES), lambda h, i: (i, nb + h // hp)),
                  pl.BlockSpec((L, LANES), lambda h, i: (0, 2 * nb + h // hp)),
                  pl.BlockSpec((L, LANES), lambda h, i: (0, 3 * nb + h // hp)),
                  pl.BlockSpec((None, nk, vd + DIFF_ONES_ROWS, tk), lambda h, i: (h, 0, 0, 0)),
                  pl.BlockSpec(diff_lam.shape, lambda h, i: (0, 0)),
                  pl.BlockSpec((vd, 1), lambda h, i: (0, 0))],
        out_specs=pl.BlockSpec((None, vd, tq), lambda h, i: (h, 0, i)),
        out_shape=jax.ShapeDtypeStruct((heads, vd, L), BF16),
        scratch_shapes=[pltpu.VMEM((2, LANES, tq), BF16), acc, acc, stat, stat],
        compiler_params=_params(2),
        name="diff_attention",
    )(qk_rot, qk_rot, qk_rot, qk_rot, vt, diff_lam, subln_g.reshape(vd, 1))
    return jnp.transpose(out_t, (2, 0, 1)).reshape(L, heads * vd)


def _s5_tables(lam_re, lam_im, log_dt, b_re, b_im, c_re, c_im, d_skip):
    T = S5_CHUNK
    _, G, P, HG = b_re.shape
    dt = jnp.exp(log_dt.astype(F32))[..., None]
    lr, li = lam_re.astype(F32), lam_im.astype(F32)
    j = jnp.arange(T + 1, dtype=F32)[:, None, None, None]
    mag = jnp.exp(lr * dt * j)
    pw_re, pw_im = mag * jnp.cos(li * dt * j), mag * jnp.sin(li * dt * j)
    den = lr * lr + li * li
    nr_, ni_ = pw_re[1] - 1.0, pw_im[1]
    f_re, f_im = (nr_ * lr + ni_ * li) / den, (ni_ * lr - nr_ * li) / den
    bb_re = f_re[..., None] * b_re - f_im[..., None] * b_im
    bb_im = f_re[..., None] * b_im + f_im[..., None] * b_re
    cr, ci = c_re.astype(F32)[None], c_im.astype(F32)[None]
    cp_re = cr * pw_re[:, :, :, None, :] - ci * pw_im[:, :, :, None, :]
    cp_im = cr * pw_im[:, :, :, None, :] + ci * pw_re[:, :, :, None, :]
    hp = lax.Precision.HIGHEST
    kj = (jnp.einsum('jdghp,dgpk->jdghk', cp_re[:T], bb_re, precision=hp)
          - jnp.einsum('jdghp,dgpk->jdghk', cp_im[:T], bb_im, precision=hp))
    sig = jnp.arange(T)[:, None]
    tau = jnp.arange(T)[None, :]
    lag_f = jnp.clip(tau - sig, 0, T - 1)
    lag_b = jnp.clip(sig - tau, 0, T - 1)
    kf = jnp.where((tau >= sig)[:, :, None, None, None], kj[lag_f, 0], 0.0)
    kb = jnp.where((sig >= tau)[:, :, None, None, None], kj[lag_b, 1], 0.0)
    kin = jnp.transpose(kf + kb, (2, 0, 4, 1, 3)).reshape(G, T * HG, T * HG)
    def bend_dir(d, order):
        pr, pi = pw_re[order, d], pw_im[order, d]
        re = pr[..., None] * bb_re[d][None] - pi[..., None] * bb_im[d][None]
        im = pr[..., None] * bb_im[d][None] + pi[..., None] * bb_re[d][None]
        to_rows = lambda t: jnp.transpose(t, (1, 0, 3, 2)).reshape(G, T * HG, P)
        return to_rows(re), to_rows(im)
    bf_re, bf_im = bend_dir(0, jnp.arange(T - 1, -1, -1))
    bb_re_, bb_im_ = bend_dir(1, jnp.arange(T))
    bend = jnp.stack([bf_re, bf_im, bb_re_, bb_im_], axis=2)
    def wout_dir(d, order):
        to_cols = lambda t: jnp.transpose(t, (1, 3, 0, 2)).reshape(G, P, T * HG)
        return to_cols(cp_re[order, d]), to_cols(-cp_im[order, d])
    wf_re, wf_im = wout_dir(0, jnp.arange(1, T + 1))
    wb_re, wb_im = wout_dir(1, jnp.arange(T, 0, -1))
    wout = jnp.stack([wf_re, wf_im, wb_re, wb_im], axis=1)
    eye2 = jnp.eye(2, dtype=F32)
    G2 = G // 2
    kin2 = jnp.einsum('karc,ab->karbc', kin.reshape(G2, 2, T * HG, T * HG), eye2)
    kin2 = kin2.reshape(G2, 2 * T * HG, 2 * T * HG)
    bend2 = jnp.einsum('karqp,ab->karqbp', bend.reshape(G2, 2, T * HG, 4, P), eye2)
    bend2 = bend2.reshape(G2, 2 * T * HG, 4 * 2 * P)
    wout2 = jnp.einsum('kaqpc,ab->kqapbc', wout.reshape(G2, 2, 4, P, T * HG), eye2)
    wout2 = wout2.reshape(G2, 4 * 2 * P, 2 * T * HG)
    a = jnp.stack([pw_re[T, 0], pw_im[T, 0], pw_re[T, 1], pw_im[T, 1]], axis=1)
    a2 = jnp.transpose(a.reshape(G2, 2, 4, P), (0, 2, 1, 3)).reshape(G2, 4, 2 * P)
    dsk = jnp.tile(d_skip.astype(F32)[:, None, :], (1, T, 1)).reshape(G2, 1, 2 * T * HG)
    return kin2.astype(BF16), bend2.astype(BF16), wout2.astype(BF16), a2, dsk


def _s5_kernel(u_ref, kin_ref, bend_ref, wout_ref, a_ref, d_ref, perm_ref, o_ref, z_sc, s_sc, g_sc):
    nc = u_ref.shape[0]
    pb, cw = kin_ref.shape[0], kin_ref.shape[1]
    sw = a_ref.shape[2]
    sub = 8
    nt = nc // sub
    row = lax.broadcasted_iota(jnp.int32, (sub, sw), 0)
    for pp in range(pb):
        u = u_ref[:, pp * cw:(pp + 1) * cw]
        ub = u.astype(BF16)
        z_sc[...] = jnp.dot(ub, bend_ref[pp], preferred_element_type=F32)
        afr, afi, abr, abi = (jnp.broadcast_to(a_ref[pp, t:t + 1, :], (sub, sw)) for t in range(4))

        def body(i, carry):
            xfr, xfi, xbr, xbi = carry
            rf = pl.multiple_of(i * sub, sub)
            rb = pl.multiple_of((nt - 1 - i) * sub, sub)
            zf = z_sc[pl.ds(rf, sub), 0:2 * sw]
            zb = z_sc[pl.ds(rb, sub), 2 * sw:4 * sw]
            sfr = sfi = sbr = sbi = jnp.zeros((sub, sw), F32)
            for jf in range(sub):
                jb = sub - 1 - jf
                sfr = jnp.where(row == jf, xfr, sfr)
                sfi = jnp.where(row == jf, xfi, sfi)
                sbr = jnp.where(row == jb, xbr, sbr)
                sbi = jnp.where(row == jb, xbi, sbi)
                zfj = jnp.broadcast_to(zf[jf:jf + 1, :], (sub, 2 * sw))
                zbj = jnp.broadcast_to(zb[jb:jb + 1, :], (sub, 2 * sw))
                xfr, xfi = (afr * xfr - afi * xfi + zfj[:, 0:sw], afr * xfi + afi * xfr + zfj[:, sw:2 * sw])
                xbr, xbi = (abr * xbr - abi * xbi + zbj[:, 0:sw], abr * xbi + abi * xbr + zbj[:, sw:2 * sw])
            s_sc[pl.ds(rf, sub), 0:sw] = sfr
            s_sc[pl.ds(rf, sub), sw:2 * sw] = sfi
            s_sc[pl.ds(rb, sub), 2 * sw:3 * sw] = sbr
            s_sc[pl.ds(rb, sub), 3 * sw:4 * sw] = sbi
            return xfr, xfi, xbr, xbi

        zero = jnp.zeros((sub, sw), F32)
        lax.fori_loop(0, nt, body, (zero, zero, zero, zero))
        y = jnp.dot(ub, kin_ref[pp], preferred_element_type=F32)
        y += jnp.dot(s_sc[...].astype(BF16), wout_ref[pp], preferred_element_type=F32)
        y += d_ref[pp] * u
        g_sc[:, pp * cw:(pp + 1) * cw] = _gelu_tanh(y).astype(BF16)
    gw = cw // 2
    ngrp = 2 * pb
    for th in range(gw // LANES):
        src = jnp.concatenate([g_sc[:, g * gw + th * LANES:g * gw + (th + 1) * LANES] for g in range(ngrp)],
                              axis=1)
        res = jnp.dot(src, perm_ref[...], preferred_element_type=F32)
        o_ref[:, th * ngrp * LANES:(th + 1) * ngrp * LANES] = res.astype(o_ref.dtype)


def _s5_bidirectional(u, tables):
    kin, bend, wout, a, dsk = tables
    L, bw = u.shape
    T = S5_CHUNK
    g2 = kin.shape[0]
    cw = kin.shape[1]
    hg = cw // (2 * T)
    pb = LANES // (2 * hg)
    nc = L // T
    assert L % T == 0 and nc % 8 == 0 and g2 * 2 * hg == bw and g2 % pb == 0 and (T * hg) % LANES == 0
    up = jnp.transpose(u.reshape(nc, T, g2, 2, hg), (0, 2, 3, 1, 4)).reshape(nc, g2 * cw)
    sw4 = bend.shape[2]
    ngrp = 2 * pb
    r = jnp.arange(ngrp * LANES)
    dst = ((r % LANES) // hg) * (ngrp * hg) + (r // LANES) * hg + r % hg
    perm = jax.nn.one_hot(dst, ngrp * LANES, dtype=BF16)
    nblk = g2 // pb
    z = pl.pallas_call(
        _s5_kernel,
        grid=(nblk,),
        in_specs=[pl.BlockSpec((nc, pb * cw), lambda g: (0, g)),
                  pl.BlockSpec((pb, cw, cw), lambda g: (g, 0, 0)),
                  pl.BlockSpec((pb, cw, sw4), lambda g: (g, 0, 0)),
                  pl.BlockSpec((pb, sw4, cw), lambda g: (g, 0, 0)),
                  pl.BlockSpec((pb, 4, sw4 // 4), lambda g: (g, 0, 0)),
                  pl.BlockSpec((pb, 1, cw), lambda g: (g, 0, 0)),
                  pl.BlockSpec((ngrp * LANES, ngrp * LANES), lambda g: (0, 0))],
        out_specs=pl.BlockSpec((nc, pb * cw), lambda g: (0, g)),
        out_shape=jax.ShapeDtypeStruct((nc, g2 * cw), BF16),
        scratch_shapes=[pltpu.VMEM((nc, sw4), F32), pltpu.VMEM((nc, sw4), F32),
                        pltpu.VMEM((nc, pb * cw), BF16)],
        compiler_params=_params(1),
        name="s5_scan",
    )(up, kin, bend, wout, a, dsk, perm)
    return jnp.transpose(z.reshape(nc, nblk, T, LANES), (0, 2, 1, 3)).reshape(L, bw)


def _encoder_trunk(x, mod, wts, prm):
    L, d = x.shape
    depth = wts["ffn_w_in"].shape[0]
    bw = wts["w_branch"].shape[2]
    hd = prm["diff_lam"].shape[-1]
    kw = (prm["na_rpb"].shape[-1] + 1) // 2
    qk_cols = wts["w_in"].shape[-1] - 5 * bw - 3 * d
    zero_bias = jnp.zeros((1, wts["ffn_w_in"].shape[-1]), F32)

    def mod_rows(l, sub):
        return tuple(mod[l, (3 * sub + t) * d:(3 * sub + t + 1) * d] for t in range(3))

    def ffn(x, l, which, sub):
        shift, scale, gate = mod_rows(l, sub)
        h = _modulate(x, prm["norm_pre"][l, sub], scale, shift)
        act = _mm_glu(h, wts["ffn_w_in"], (l, which), zero_bias, True, "ffn_up")
        y = _mm(act, wts["ffn_w_out"], (l, which), 0, d, F32, tm_pref=512, tn_pref=256, name="ffn_down")
        return _residual(x, y, prm["norm_post"][l, sub], gate, 0.5)

    for l in range(depth):
        lam_init = 0.8 - 0.6 * math.exp(-0.3 * l)
        x = ffn(x, l, 0, 0)
        shift, scale, gate = mod_rows(l, 1)
        h = _modulate(x, prm["norm_pre"][l, 1], scale, shift)
        w_in = wts["w_in"]
        qkv_a = _mm(h, w_in, (l,), 0, 3 * bw, BF16, name="in_proj_na")
        u = _mm(h, w_in, (l,), 3 * bw, bw, F32, name="in_proj_s5")
        qk_rot = _mm_rotary(h, w_in, (l,), 4 * bw, qk_cols, hd)
        vc = _mm(h, w_in, (l,), 4 * bw + qk_cols, bw, BF16, name="in_proj_v")
        gates = _mm(h, w_in, (l,), 5 * bw + qk_cols, 3 * d, F32, name="in_proj_gates")
        y_a = _neighborhood_attention(qkv_a, prm["na_rpb"][l], kw)
        z = _s5_bidirectional(u, prm["s5_tables"][l])
        y_b = _mm_glu(z, wts["s5_w_glu"], (l,), prm["s5_b_glu"][l].reshape(1, -1), False, "s5_glu")
        y_c = _diff_attention(qk_rot, vc, prm["diff_lam"][l], prm["diff_subln"][l], lam_init)
        merged = _merge(y_a, y_b, y_c, wts["w_branch"], l, gates)
        m = _mm(merged, wts["w_out"], (l,), 0, d, F32, name="out_proj")
        x = _residual(x, m, prm["norm_post"][l, 1], gate, 1.0)
        x = ffn(x, l, 1, 2)
    return x


def kernel(x_prompt, x_sample, c_prompt, c_sample, ada_w, ada_b, norm_pre, norm_post, ffn_w_in, ffn_w_out,
           w_in, w_branch, w_out, na_rpb, s5_lam_re, s5_lam_im, s5_log_dt, s5_b_re, s5_b_im, s5_c_re,
           s5_c_im, s5_d, s5_w_glu, s5_b_glu, diff_lam, diff_subln):
    depth = ada_w.shape[0]
    wts = {"ffn_w_in": ffn_w_in.astype(BF16), "ffn_w_out": ffn_w_out.astype(BF16),
           "w_in": w_in.astype(BF16), "w_branch": w_branch.astype(BF16),
           "w_out": w_out.astype(BF16), "s5_w_glu": s5_w_glu.astype(BF16)}
    prm = {"norm_pre": norm_pre, "norm_post": norm_post, "na_rpb": na_rpb, "s5_b_glu": s5_b_glu,
           "diff_lam": diff_lam, "diff_subln": diff_subln,
           "s5_tables": [_s5_tables(s5_lam_re[l], s5_lam_im[l], s5_log_dt[l], s5_b_re[l], s5_b_im[l],
                                    s5_c_re[l], s5_c_im[l], s5_d[l]) for l in range(depth)]}
    c_all = jnp.concatenate([c_prompt, c_sample], axis=0)
    mod = _ada_mod(c_all, ada_w, ada_b)
    outs = []
    row = 0
    for x in (x_prompt, x_sample):
        ys = []
        for b in range(x.shape[0]):
            ys.append(_encoder_trunk(x[b], mod[:, row], wts, prm))
            row += 1
        outs.append(ys[0][None] if len(ys) == 1 else jnp.stack(ys, axis=0))
    return tuple(outs)
```

```python
import functools
import math

import jax
import jax.numpy as jnp
from jax import lax
from jax.experimental import pallas as pl
from jax.experimental.pallas import tpu as pltpu

F32 = jnp.float32
BF16 = jnp.bfloat16

GRID_W = 64
ROPE_THETA = 10000.0
RMS_EPS = 1e-6
NEG_INF = -1e30
LANES = 128
S5_CHUNK = 16
VMEM_LIMIT_BYTES = 56 * 1024 * 1024
GLU_TM = 2048
FFN_DOWN_TM = 512
FFN_DOWN_TN = 512


def _params(ndims):
    return pltpu.CompilerParams(dimension_semantics=("arbitrary",) * ndims,
                                vmem_limit_bytes=VMEM_LIMIT_BYTES)


def _pick(n, *cands):
    for c in cands:
        if n % c == 0:
            return c
    return n


def _silu(x):
    return x * jax.nn.sigmoid(x)


def _gelu_tanh(x):
    return 0.5 * x * (1.0 + jnp.tanh(math.sqrt(2.0 / math.pi) * (x + 0.044715 * (x * x * x))))


def _ada_kernel(c_ref, w_ref, b_ref, o_ref, *, kc):
    d, nb = c_ref.shape
    tn = w_ref.shape[1]

    def body(k, accs):
        off = pl.multiple_of(k * kc, kc)
        wk = w_ref[pl.ds(off, kc), :]
        ck = _silu(c_ref[pl.ds(off, kc), :])
        return tuple(acc + jnp.sum(wk * ck[:, b:b + 1], axis=0, keepdims=True)
                     for b, acc in enumerate(accs))

    accs = lax.fori_loop(0, d // kc, body, tuple(jnp.zeros((1, tn), F32) for _ in range(nb)))
    for b in range(nb):
        o_ref[b:b + 1, :] = accs[b] + b_ref[...]


def _ada_mod(c_all, ada_w, ada_b):
    nb, d = c_all.shape
    depth, _, n = ada_w.shape
    tn = _pick(n, 512, 256, 128)
    kc = _pick(d, 256, 128, 8)
    return pl.pallas_call(
        functools.partial(_ada_kernel, kc=kc),
        grid=(depth, n // tn),
        in_specs=[pl.BlockSpec((d, nb), lambda l, j: (0, 0)),
                  pl.BlockSpec((None, d, tn), lambda l, j: (l, 0, j)),
                  pl.BlockSpec((None, 1, tn), lambda l, j: (l, 0, j))],
        out_specs=pl.BlockSpec((None, nb, tn), lambda l, j: (l, 0, j)),
        out_shape=jax.ShapeDtypeStruct((depth, nb, n), F32),
        compiler_params=_params(2),
        name="ada_mod",
    )(c_all.T, ada_w, ada_b.reshape(depth, 1, n))


def _modulate_kernel(x_ref, g_ref, sc_ref, sh_ref, o_ref):
    x = x_ref[...]
    y = x * lax.rsqrt(jnp.mean(x * x, axis=-1, keepdims=True) + RMS_EPS) * g_ref[...]
    o_ref[...] = (y * (1.0 + sc_ref[...]) + sh_ref[...]).astype(o_ref.dtype)


def _modulate(x, g, scale, shift):
    m, d = x.shape
    tm = _pick(m, 256, 128, 8)
    row = pl.BlockSpec((1, d), lambda i: (0, 0))
    return pl.pallas_call(
        _modulate_kernel,
        grid=(m // tm,),
        in_specs=[pl.BlockSpec((tm, d), lambda i: (i, 0)), row, row, row],
        out_specs=pl.BlockSpec((tm, d), lambda i: (i, 0)),
        out_shape=jax.ShapeDtypeStruct((m, d), BF16),
        compiler_params=_params(1),
        name="modulate",
    )(x, g.reshape(1, d), scale.reshape(1, d), shift.reshape(1, d))


def _residual_kernel(x_ref, y_ref, g_ref, gate_ref, o_ref, *, coef):
    y = y_ref[...]
    n = y * lax.rsqrt(jnp.mean(y * y, axis=-1, keepdims=True) + RMS_EPS) * g_ref[...]
    o_ref[...] = x_ref[...] + (coef * gate_ref[...]) * n


def _residual_modulate_kernel(x_ref, y_ref, g_ref, gate_ref, g2_ref, sc_ref, sh_ref, o_ref, h_ref, *, coef):
    y = y_ref[...]
    n = y * lax.rsqrt(jnp.mean(y * y, axis=-1, keepdims=True) + RMS_EPS) * g_ref[...]
    x = x_ref[...] + (coef * gate_ref[...]) * n
    o_ref[...] = x
    hn = x * lax.rsqrt(jnp.mean(x * x, axis=-1, keepdims=True) + RMS_EPS) * g2_ref[...]
    h_ref[...] = (hn * (1.0 + sc_ref[...]) + sh_ref[...]).astype(h_ref.dtype)


def _residual(x, y, g, gate, coef, next_mod=None):
    m, d = x.shape
    tm = _pick(m, 256, 128, 8)
    row = pl.BlockSpec((1, d), lambda i: (0, 0))
    blk = pl.BlockSpec((tm, d), lambda i: (i, 0))
    if next_mod is None:
        return pl.pallas_call(
            functools.partial(_residual_kernel, coef=coef),
            grid=(m // tm,),
            in_specs=[blk, blk, row, row],
            out_specs=blk,
            out_shape=jax.ShapeDtypeStruct((m, d), F32),
            compiler_params=_params(1),
            name="residual",
        )(x, y, g.reshape(1, d), gate.reshape(1, d)), None
    return pl.pallas_call(
        functools.partial(_residual_modulate_kernel, coef=coef),
        grid=(m // tm,),
        in_specs=[blk, blk, row, row, row, row, row],
        out_specs=(blk, blk),
        out_shape=(jax.ShapeDtypeStruct((m, d), F32), jax.ShapeDtypeStruct((m, d), BF16)),
        compiler_params=_params(1),
        name="residual_modulate",
    )(x, y, g.reshape(1, d), gate.reshape(1, d), *[t.reshape(1, d) for t in next_mod])


def _mm_kernel(a_ref, w_ref, o_ref, *, sigmoid):
    y = jnp.dot(a_ref[...], w_ref[...], preferred_element_type=F32)
    o_ref[...] = (jax.nn.sigmoid(y) if sigmoid else y).astype(o_ref.dtype)


def _mm(a, w, lead, col_off, ncols, out_dtype, tm_pref=1024, tn_pref=1024, sigmoid=False, name="mm"):
    m, k = a.shape
    tm = _pick(m, *[c for c in (1024, 512, 256, 128, 8) if c <= tm_pref])
    tn = _pick(math.gcd(col_off, ncols), *[c for c in (1024, 768, 512, 384, 256, 128) if c <= tn_pref])
    off = col_off // tn
    w_spec = pl.BlockSpec((None,) * len(lead) + (k, tn), lambda i, j: tuple(lead) + (0, off + j))
    return pl.pallas_call(
        functools.partial(_mm_kernel, sigmoid=sigmoid),
        grid=(m // tm, ncols // tn),
        in_specs=[pl.BlockSpec((tm, k), lambda i, j: (i, 0)), w_spec],
        out_specs=pl.BlockSpec((tm, tn), lambda i, j: (i, j)),
        out_shape=jax.ShapeDtypeStruct((m, ncols), out_dtype),
        compiler_params=_params(2),
        name=name,
    )(a, w)


def _mm_glu_kernel(a_ref, w1_ref, w2_ref, b1_ref, b2_ref, o_ref, *, swiglu):
    a = a_ref[...].astype(BF16)
    y1 = jnp.dot(a, w1_ref[...], preferred_element_type=F32) + b1_ref[...]
    y2 = jnp.dot(a, w2_ref[...], preferred_element_type=F32) + b2_ref[...]
    out = _silu(y1) * y2 if swiglu else y1 * jax.nn.sigmoid(y2)
    o_ref[...] = out.astype(o_ref.dtype)


def _mm_glu(a, w, lead, bias, swiglu, name):
    m, k = a.shape
    half = w.shape[-1] // 2
    tm = _pick(m, *[c for c in (2048, 1024, 512, 256, 128, 8) if c <= GLU_TM])
    tn = _pick(half, 512, 256, 128)
    nj = half // tn
    nl = len(lead)
    w1 = pl.BlockSpec((None,) * nl + (k, tn), lambda i, j: tuple(lead) + (0, j))
    w2 = pl.BlockSpec((None,) * nl + (k, tn), lambda i, j: tuple(lead) + (0, nj + j))
    b1 = pl.BlockSpec((1, tn), lambda i, j: (0, j))
    b2 = pl.BlockSpec((1, tn), lambda i, j: (0, nj + j))
    return pl.pallas_call(
        functools.partial(_mm_glu_kernel, swiglu=swiglu),
        grid=(m // tm, nj),
        in_specs=[pl.BlockSpec((tm, k), lambda i, j: (i, 0)), w1, w2, b1, b2],
        out_specs=pl.BlockSpec((tm, tn), lambda i, j: (i, j)),
        out_shape=jax.ShapeDtypeStruct((m, half), BF16),
        compiler_params=_params(2),
        name=name,
    )(a, w, w, bias, bias)


def _merge_kernel(ya_ref, yb_ref, yc_ref, w_ref, ga_ref, gb_ref, gc_ref, o_ref):
    acc = ga_ref[...].astype(F32) * jnp.dot(ya_ref[...], w_ref[0], preferred_element_type=F32)
    acc += gb_ref[...].astype(F32) * jnp.dot(yb_ref[...], w_ref[1], preferred_element_type=F32)
    acc += gc_ref[...].astype(F32) * jnp.dot(yc_ref[...], w_ref[2], preferred_element_type=F32)
    o_ref[...] = acc.astype(o_ref.dtype)


def _merge(ya, yb, yc, w_branch, l, gates):
    m, bw = ya.shape
    d = w_branch.shape[-1]
    tm = _pick(m, 512, 256, 128, 8)
    tn = _pick(d, 512, 256, 128)
    nj = d // tn
    y_spec = pl.BlockSpec((tm, bw), lambda i, j: (i, 0))
    g_specs = [pl.BlockSpec((tm, tn), functools.partial(lambda i, j, k: (i, k * nj + j), k=k))
               for k in range(3)]
    return pl.pallas_call(
        _merge_kernel,
        grid=(m // tm, nj),
        in_specs=[y_spec, y_spec, y_spec,
                  pl.BlockSpec((None, 3, bw, tn), lambda i, j: (l, 0, 0, j))] + g_specs,
        out_specs=pl.BlockSpec((tm, tn), lambda i, j: (i, j)),
        out_shape=jax.ShapeDtypeStruct((m, d), BF16),
        compiler_params=_params(2),
        name="branch_merge",
    )(ya, yb, yc, w_branch, gates, gates, gates)


def _na_bias_kernel(rpb_ref, o_ref, *, kw, nr, nc):
    h = pl.program_id(0)
    w = GRID_W
    q = lax.broadcasted_iota(jnp.int32, (w, 2 * w), 0)
    c2 = lax.broadcasted_iota(jnp.int32, (w, 2 * w), 1)
    second = c2 >= w
    c = jnp.where(second, c2 - w, c2)
    dc = jnp.clip(c - q, -(kw - 1), kw - 1) + (kw - 1)
    cs = jnp.clip(q - kw // 2, 0, w - kw)
    ok = jnp.logical_and(c >= cs, c < cs + kw)
    for dr in range(nr - 1):
        t = jnp.zeros((w, 2 * w), F32)
        for dd in range(nc):
            lo = rpb_ref[(h * nr + dr) * nc + dd]
            hi = rpb_ref[(h * nr + dr + 1) * nc + dd]
            t = jnp.where(dc == dd, jnp.where(second, hi, lo), t)
        o_ref[dr] = jnp.where(ok, t, NEG_INF)


def _na_bias(rpb, kw):
    heads, nr, nc = rpb.shape
    return pl.pallas_call(
        functools.partial(_na_bias_kernel, kw=kw, nr=nr, nc=nc),
        grid=(heads,),
        in_specs=[pl.BlockSpec(memory_space=pltpu.SMEM)],
        out_specs=pl.BlockSpec((None, nr - 1, GRID_W, 2 * GRID_W), lambda h: (h, 0, 0, 0)),
        out_shape=jax.ShapeDtypeStruct((heads, nr - 1, GRID_W, 2 * GRID_W), F32),
        compiler_params=_params(1),
        name="na_bias",
    )(rpb.reshape(-1))


def _na_kernel(q_ref, k_ref, v_ref, tab_ref, o_ref, *, rb, rows, kh, kh_full, scale):
    i = pl.program_id(1)
    w = GRID_W
    scores, toks = [], []
    for rr in range(rb):
        r = i * rb + rr
        start = jnp.clip(r - kh // 2, 0, rows - kh)
        dr0 = start - r + (kh_full - 1)
        tok = pl.multiple_of(start * w, w)
        q = q_ref[rr * w:(rr + 1) * w, :]
        kwin = k_ref[pl.ds(tok, kh * w), :]
        s = lax.dot_general(q, kwin, (((1,), (1,)), ((), ())), preferred_element_type=F32) * scale
        bias = jnp.concatenate([tab_ref[dr0 + 2 * mm] for mm in range(kh // 2)], axis=1)
        scores.append(s + bias)
        toks.append(tok)
    for rr in range(rb):
        s = scores[rr]
        p = jnp.exp(s - jnp.max(s, axis=-1, keepdims=True))
        denom = jnp.sum(p, axis=-1, keepdims=True)
        o = jnp.dot(p.astype(BF16), v_ref[pl.ds(toks[rr], kh * w), :], preferred_element_type=F32)
        o_ref[rr * w:(rr + 1) * w, :] = (o / denom).astype(o_ref.dtype)


def _neighborhood_attention(qkv, rpb, kw):
    L = qkv.shape[0]
    heads, nr, _ = rpb.shape
    bw = qkv.shape[1] // 3
    hd = bw // heads
    assert hd == LANES, "neighbourhood-attention head dim must fill one lane tile"
    rows = L // GRID_W
    kh_full = (nr + 1) // 2
    kh = min(kh_full, rows)
    assert kh % 2 == 0
    tab = _na_bias(rpb, kw)
    rb = _pick(rows, 8, 4, 2, 1)
    return pl.pallas_call(
        functools.partial(_na_kernel, rb=rb, rows=rows, kh=kh, kh_full=kh_full, scale=hd ** -0.5),
        grid=(heads, rows // rb),
        in_specs=[pl.BlockSpec((rb * GRID_W, hd), lambda h, i: (i, h)),
                  pl.BlockSpec((L, hd), lambda h, i: (0, heads + h)),
                  pl.BlockSpec((L, hd), lambda h, i: (0, 2 * heads + h)),
                  pl.BlockSpec((None, nr - 1, GRID_W, 2 * GRID_W), lambda h, i: (h, 0, 0, 0))],
        out_specs=pl.BlockSpec((rb * GRID_W, hd), lambda h, i: (i, h)),
        out_shape=jax.ShapeDtypeStruct((L, bw), BF16),
        compiler_params=_params(2),
        name="neighborhood_attention",
    )(qkv, qkv, qkv, tab)


def _mm_rotary_kernel(a_ref, w_ref, cos_ref, sin_ref, o_ref, *, hd, n_q_tiles, q_scale):
    y = jnp.dot(a_ref[...], w_ref[...], preferred_element_type=F32)
    scale = jnp.where(pl.program_id(1) < n_q_tiles, q_scale, 1.0)
    cos, sin = cos_ref[...] * scale, sin_ref[...] * scale
    lane = lax.broadcasted_iota(jnp.int32, cos.shape, 1)
    first = (lane % hd) < (hd // 2)
    for t in range(y.shape[1] // LANES):
        x = y[:, t * LANES:(t + 1) * LANES]
        rot = jnp.where(first, pltpu.roll(x, LANES - hd // 2, 1), pltpu.roll(x, hd // 2, 1))
        o_ref[:, t * LANES:(t + 1) * LANES] = (x * cos + rot * sin).astype(o_ref.dtype)


def _mm_rotary(a, w, lead, col_off, ncols, hd):
    L, k = a.shape
    assert LANES % hd == 0
    half = hd // 2
    inv_freq = ROPE_THETA ** (-jnp.arange(half, dtype=F32) / half)
    ang = jnp.arange(L, dtype=F32)[:, None] * inv_freq[None, :]
    cos, sin = jnp.cos(ang), jnp.sin(ang)
    reps = LANES // hd
    cos_t = jnp.tile(jnp.concatenate([cos, cos], axis=1), (1, reps))
    sin_t = jnp.tile(jnp.concatenate([-sin, sin], axis=1), (1, reps))
    tm = _pick(L, 1024, 512, 256, 128, 8)
    tn = _pick(math.gcd(col_off, ncols // 2), 768, 512, 384, 256, 128)
    off = col_off // tn
    tab = pl.BlockSpec((tm, LANES), lambda i, j: (i, 0))
    return pl.pallas_call(
        functools.partial(_mm_rotary_kernel, hd=hd, n_q_tiles=ncols // 2 // tn,
                          q_scale=hd ** -0.5 * math.log2(math.e)),
        grid=(L // tm, ncols // tn),
        in_specs=[pl.BlockSpec((tm, k), lambda i, j: (i, 0)),
                  pl.BlockSpec((None,) * len(lead) + (k, tn), lambda i, j: tuple(lead) + (0, off + j)),
                  tab, tab],
        out_specs=pl.BlockSpec((tm, tn), lambda i, j: (i, j)),
        out_shape=jax.ShapeDtypeStruct((L, ncols), BF16),
        compiler_params=_params(2),
        name="in_proj_qk_rotary",
    )(a, w, cos_t, sin_t)


DIFF_TQ = 256
DIFF_TK = 8192
DIFF_SUB = 256
DIFF_LOOKAHEAD = 5
DIFF_ONES_ROWS = 16


def _diff_attn_kernel(q1_ref, q2_ref, k1_ref, k2_ref, v_ref, lam_ref, g_ref, o_ref,
                      qp_sc, acc1, acc2, m1, m2, *, tk, sub, hd, vd, lam_init):
    tq = q1_ref.shape[0]
    nk = v_ref.shape[0]
    par = pl.program_id(0) % (LANES // hd)
    rowi = lax.broadcasted_iota(jnp.int32, (LANES, tq), 0)
    keep = jnp.logical_and(rowi >= par * hd, rowi < (par + 1) * hd)
    for t, q_ref in enumerate((q1_ref, q2_ref)):
        qp_sc[t] = jnp.where(keep, q_ref[...].astype(F32).T, 0.0).astype(BF16)
    for acc, m in ((acc1, m1), (acc2, m2)):
        acc[...] = jnp.zeros(acc.shape, F32)
        m[...] = jnp.full(m.shape, NEG_INF, F32)

    def body(c, carry):
        off = pl.multiple_of(c * tk, tk)
        tasks = [(j, t) for j in range(tk // sub) for t in range(2)]
        k_refs, accs, ms = (k1_ref, k2_ref), (acc1, acc2), (m1, m2)

        def scores(j, t):
            ks = k_refs[t][pl.ds(off + j * sub, sub), :]
            return jnp.dot(ks, qp_sc[t], preferred_element_type=F32)

        pending = [scores(*task) for task in tasks[:DIFF_LOOKAHEAD]]
        for idx, (j, t) in enumerate(tasks):
            s = pending.pop(0)
            if idx + DIFF_LOOKAHEAD < len(tasks):
                pending.append(scores(*tasks[idx + DIFF_LOOKAHEAD]))
            m_old = ms[t][...]
            m_new = jnp.maximum(m_old, jnp.max(s, axis=0, keepdims=True))
            p = jnp.exp2(s - m_new).astype(BF16)
            pv = jnp.dot(v_ref[c, :, j * sub:(j + 1) * sub], p, preferred_element_type=F32)
            accs[t][...] = jnp.exp2(m_old - m_new) * accs[t][...] + pv
            ms[t][...] = m_new
        return carry

    lax.fori_loop(0, nk, body, 0)
    lp = lam_ref[...]
    lam = (jnp.exp(jnp.sum(lp[0:1] * lp[1:2], axis=-1, keepdims=True))
           - jnp.exp(jnp.sum(lp[2:3] * lp[3:4], axis=-1, keepdims=True)) + lam_init)
    a1, a2 = acc1[...], acc2[...]
    o = a1[:vd] / a1[vd:vd + 1] - lam * (a2[:vd] / a2[vd:vd + 1])
    o = o * lax.rsqrt(jnp.mean(o * o, axis=0, keepdims=True) + RMS_EPS) * g_ref[...]
    o_ref[...] = (o * (1.0 - lam_init)).astype(o_ref.dtype)


def _diff_attention(qk_rot, v, diff_lam, subln_g, lam_init):
    L = qk_rot.shape[0]
    hd = diff_lam.shape[-1]
    heads = qk_rot.shape[1] // (4 * hd)
    vd = v.shape[1] // heads
    assert vd == LANES and LANES % hd == 0 and (heads * hd) % LANES == 0
    hp = LANES // hd
    nb = heads * hd // LANES
    tq = _pick(L, DIFF_TQ, 128)
    tk = _pick(L, DIFF_TK, 4096, 2048, 1024, 512, 256, 128)
    nk = L // tk
    vt = jnp.transpose(v.reshape(nk, tk, heads, vd), (2, 0, 3, 1))
    vt = jnp.concatenate([vt, jnp.ones((heads, nk, DIFF_ONES_ROWS, tk), BF16)], axis=2)
    acc = pltpu.VMEM((vd + DIFF_ONES_ROWS, tq), F32)
    stat = pltpu.VMEM((1, tq), F32)
    out_t = pl.pallas_call(
        functools.partial(_diff_attn_kernel, tk=tk, sub=_pick(tk, DIFF_SUB), hd=hd, vd=vd, lam_init=lam_init),
        grid=(heads, L // tq),
        in_specs=[pl.BlockSpec((tq, LANES), lambda h, i: (i, h // hp)),
                  pl.BlockSpec((tq, LANES), lambda h, i: (i, nb + h // hp)),
                  pl.BlockSpec((L, LANES), lambda h, i: (0, 2 * nb + h // hp)),
                  pl.BlockSpec((L, LANES), lambda h, i: (0, 3 * nb + h // hp)),
                  pl.BlockSpec((None, nk, vd + DIFF_ONES_ROWS, tk), lambda h, i: (h, 0, 0, 0)),
                  pl.BlockSpec(diff_lam.shape, lambda h, i: (0, 0)),
                  pl.BlockSpec((vd, 1), lambda h, i: (0, 0))],
        out_specs=pl.BlockSpec((None, vd, tq), lambda h, i: (h, 0, i)),
        out_shape=jax.ShapeDtypeStruct((heads, vd, L), BF16),
        scratch_shapes=[pltpu.VMEM((2, LANES, tq), BF16), acc, acc, stat, stat],
        compiler_params=_params(2),
        name="diff_attention",
    )(qk_rot, qk_rot, qk_rot, qk_rot, vt, diff_lam, subln_g.reshape(vd, 1))
    return jnp.transpose(out_t, (2, 0, 1)).reshape(L, heads * vd)


def _s5_tables(lam_re, lam_im, log_dt, b_re, b_im, c_re, c_im, d_skip):
    T = S5_CHUNK
    _, G, P, HG = b_re.shape
    dt = jnp.exp(log_dt.astype(F32))[..., None]
    lr, li = lam_re.astype(F32), lam_im.astype(F32)
    j = jnp.arange(T + 1, dtype=F32)[:, None, None, None]
    mag = jnp.exp(lr * dt * j)
    pw_re, pw_im = mag * jnp.cos(li * dt * j), mag * jnp.sin(li * dt * j)
    den = lr * lr + li * li
    nr_, ni_ = pw_re[1] - 1.0, pw_im[1]
    f_re, f_im = (nr_ * lr + ni_ * li) / den, (ni_ * lr - nr_ * li) / den
    bb_re = f_re[..., None] * b_re - f_im[..., None] * b_im
    bb_im = f_re[..., None] * b_im + f_im[..., None] * b_re
    cr, ci = c_re.astype(F32)[None], c_im.astype(F32)[None]
    cp_re = cr * pw_re[:, :, :, None, :] - ci * pw_im[:, :, :, None, :]
    cp_im = cr * pw_im[:, :, :, None, :] + ci * pw_re[:, :, :, None, :]
    hp = lax.Precision.HIGHEST
    kj = (jnp.einsum('jdghp,dgpk->jdghk', cp_re[:T], bb_re, precision=hp)
          - jnp.einsum('jdghp,dgpk->jdghk', cp_im[:T], bb_im, precision=hp))
    sig = jnp.arange(T)[:, None]
    tau = jnp.arange(T)[None, :]
    lag_f = jnp.clip(tau - sig, 0, T - 1)
    lag_b = jnp.clip(sig - tau, 0, T - 1)
    kf = jnp.where((tau >= sig)[:, :, None, None, None], kj[lag_f, 0], 0.0)
    kb = jnp.where((sig >= tau)[:, :, None, None, None], kj[lag_b, 1], 0.0)
    kin = jnp.transpose(kf + kb, (2, 0, 4, 1, 3)).reshape(G, T * HG, T * HG)
    def bend_dir(d, order):
        pr, pi = pw_re[order, d], pw_im[order, d]
        re = pr[..., None] * bb_re[d][None] - pi[..., None] * bb_im[d][None]
        im = pr[..., None] * bb_im[d][None] + pi[..., None] * bb_re[d][None]
        to_rows = lambda t: jnp.transpose(t, (1, 0, 3, 2)).reshape(G, T * HG, P)
        return to_rows(re), to_rows(im)
    bf_re, bf_im = bend_dir(0, jnp.arange(T - 1, -1, -1))
    bb_re_, bb_im_ = bend_dir(1, jnp.arange(T))
    bend = jnp.stack([bf_re, bf_im, bb_re_, bb_im_], axis=2)
    def wout_dir(d, order):
        to_cols = lambda t: jnp.transpose(t, (1, 3, 0, 2)).reshape(G, P, T * HG)
        return to_cols(cp_re[order, d]), to_cols(-cp_im[order, d])
    wf_re, wf_im = wout_dir(0, jnp.arange(1, T + 1))
    wb_re, wb_im = wout_dir(1, jnp.arange(T, 0, -1))
    wout = jnp.stack([wf_re, wf_im, wb_re, wb_im], axis=1)
    eye2 = jnp.eye(2, dtype=F32)
    G2 = G // 2
    kin2 = jnp.einsum('karc,ab->karbc', kin.reshape(G2, 2, T * HG, T * HG), eye2)
    kin2 = kin2.reshape(G2, 2 * T * HG, 2 * T * HG)
    bend2 = jnp.einsum('karqp,ab->karqbp', bend.reshape(G2, 2, T * HG, 4, P), eye2)
    bend2 = bend2.reshape(G2, 2 * T * HG, 4 * 2 * P)
    wout2 = jnp.einsum('kaqpc,ab->kqapbc', wout.reshape(G2, 2, 4, P, T * HG), eye2)
    wout2 = wout2.reshape(G2, 4 * 2 * P, 2 * T * HG)
    a = jnp.stack([pw_re[T, 0], pw_im[T, 0], pw_re[T, 1], pw_im[T, 1]], axis=1)
    a2 = jnp.transpose(a.reshape(G2, 2, 4, P), (0, 2, 1, 3)).reshape(G2, 4, 2 * P)
    dsk = jnp.tile(d_skip.astype(F32)[:, None, :], (1, T, 1)).reshape(G2, 1, 2 * T * HG)
    return kin2.astype(BF16), bend2.astype(BF16), wout2.astype(BF16), a2, dsk


def _s5_kernel(u_ref, kin_ref, bend_ref, wout_ref, a_ref, d_ref, perm_ref, o_ref, z_sc, s_sc, g_sc):
    nc = u_ref.shape[0]
    pb, cw = kin_ref.shape[0], kin_ref.shape[1]
    sw = a_ref.shape[2]
    sub = 8
    nt = nc // sub
    row = lax.broadcasted_iota(jnp.int32, (sub, sw), 0)
    for pp in range(pb):
        u = u_ref[:, pp * cw:(pp + 1) * cw]
        ub = u.astype(BF16)
        z_sc[...] = jnp.dot(ub, bend_ref[pp], preferred_element_type=F32)
        afr, afi, abr, abi = (jnp.broadcast_to(a_ref[pp, t:t + 1, :], (sub, sw)) for t in range(4))

        def body(i, carry):
            xfr, xfi, xbr, xbi = carry
            rf = pl.multiple_of(i * sub, sub)
            rb = pl.multiple_of((nt - 1 - i) * sub, sub)
            zf = z_sc[pl.ds(rf, sub), 0:2 * sw]
            zb = z_sc[pl.ds(rb, sub), 2 * sw:4 * sw]
            sfr = sfi = sbr = sbi = jnp.zeros((sub, sw), F32)
            for jf in range(sub):
                jb = sub - 1 - jf
                sfr = jnp.where(row == jf, xfr, sfr)
                sfi = jnp.where(row == jf, xfi, sfi)
                sbr = jnp.where(row == jb, xbr, sbr)
                sbi = jnp.where(row == jb, xbi, sbi)
                zfj = jnp.broadcast_to(zf[jf:jf + 1, :], (sub, 2 * sw))
                zbj = jnp.broadcast_to(zb[jb:jb + 1, :], (sub, 2 * sw))
                xfr, xfi = (afr * xfr - afi * xfi + zfj[:, 0:sw], afr * xfi + afi * xfr + zfj[:, sw:2 * sw])
                xbr, xbi = (abr * xbr - abi * xbi + zbj[:, 0:sw], abr * xbi + abi * xbr + zbj[:, sw:2 * sw])
            s_sc[pl.ds(rf, sub), 0:sw] = sfr
            s_sc[pl.ds(rf, sub), sw:2 * sw] = sfi
            s_sc[pl.ds(rb, sub), 2 * sw:3 * sw] = sbr
            s_sc[pl.ds(rb, sub), 3 * sw:4 * sw] = sbi
            return xfr, xfi, xbr, xbi

        zero = jnp.zeros((sub, sw), F32)
        lax.fori_loop(0, nt, body, (zero, zero, zero, zero))
        y = jnp.dot(ub, kin_ref[pp], preferred_element_type=F32)
        y += jnp.dot(s_sc[...].astype(BF16), wout_ref[pp], preferred_element_type=F32)
        y += d_ref[pp] * u
        g_sc[:, pp * cw:(pp + 1) * cw] = _gelu_tanh(y).astype(BF16)
    gw = cw // 2
    ngrp = 2 * pb
    for th in range(gw // LANES):
        src = jnp.concatenate([g_sc[:, g * gw + th * LANES:g * gw + (th + 1) * LANES] for g in range(ngrp)],
                              axis=1)
        res = jnp.dot(src, perm_ref[...], preferred_element_type=F32)
        o_ref[:, th * ngrp * LANES:(th + 1) * ngrp * LANES] = res.astype(o_ref.dtype)


def _s5_bidirectional(u, tables):
    kin, bend, wout, a, dsk = tables
    L, bw = u.shape
    T = S5_CHUNK
    g2 = kin.shape[0]
    cw = kin.shape[1]
    hg = cw // (2 * T)
    pb = LANES // (2 * hg)
    nc = L // T
    assert L % T == 0 and nc % 8 == 0 and g2 * 2 * hg == bw and g2 % pb == 0 and (T * hg) % LANES == 0
    up = jnp.transpose(u.reshape(nc, T, g2, 2, hg), (0, 2, 3, 1, 4)).reshape(nc, g2 * cw)
    sw4 = bend.shape[2]
    ngrp = 2 * pb
    r = jnp.arange(ngrp * LANES)
    dst = ((r % LANES) // hg) * (ngrp * hg) + (r // LANES) * hg + r % hg
    perm = jax.nn.one_hot(dst, ngrp * LANES, dtype=BF16)
    nblk = g2 // pb
    z = pl.pallas_call(
        _s5_kernel,
        grid=(nblk,),
        in_specs=[pl.BlockSpec((nc, pb * cw), lambda g: (0, g)),
                  pl.BlockSpec((pb, cw, cw), lambda g: (g, 0, 0)),
                  pl.BlockSpec((pb, cw, sw4), lambda g: (g, 0, 0)),
                  pl.BlockSpec((pb, sw4, cw), lambda g: (g, 0, 0)),
                  pl.BlockSpec((pb, 4, sw4 // 4), lambda g: (g, 0, 0)),
                  pl.BlockSpec((pb, 1, cw), lambda g: (g, 0, 0)),
                  pl.BlockSpec((ngrp * LANES, ngrp * LANES), lambda g: (0, 0))],
        out_specs=pl.BlockSpec((nc, pb * cw), lambda g: (0, g)),
        out_shape=jax.ShapeDtypeStruct((nc, g2 * cw), BF16),
        scratch_shapes=[pltpu.VMEM((nc, sw4), F32), pltpu.VMEM((nc, sw4), F32),
                        pltpu.VMEM((nc, pb * cw), BF16)],
        compiler_params=_params(1),
        name="s5_scan",
    )(up, kin, bend, wout, a, dsk, perm)
    return jnp.transpose(z.reshape(nc, nblk, T, LANES), (0, 2, 1, 3)).reshape(L, bw)


def _encoder_trunk(x, mod, wts, prm):
    L, d = x.shape
    depth = wts["ffn_w_in"].shape[0]
    bw = wts["w_branch"].shape[2]
    hd = prm["diff_lam"].shape[-1]
    kw = (prm["na_rpb"].shape[-1] + 1) // 2
    qk_cols = wts["w_in"].shape[-1] - 5 * bw - 3 * d
    zero_bias = jnp.zeros((1, wts["ffn_w_in"].shape[-1]), F32)

    def mod_rows(l, sub):
        return tuple(mod[l, (3 * sub + t) * d:(3 * sub + t + 1) * d] for t in range(3))

    def next_mod(l, sub):
        l, sub = (l, sub + 1) if sub < 2 else (l + 1, 0)
        if l == depth:
            return None
        shift, scale, _ = mod_rows(l, sub)
        return prm["norm_pre"][l, sub], scale, shift

    def ffn(x, h, l, which, sub):
        gate = mod_rows(l, sub)[2]
        act = _mm_glu(h, wts["ffn_w_in"], (l, which), zero_bias, True, "ffn_up")
        y = _mm(act, wts["ffn_w_out"], (l, which), 0, d, F32, tm_pref=FFN_DOWN_TM, tn_pref=FFN_DOWN_TN,
                name="ffn_down")
        return _residual(x, y, prm["norm_post"][l, sub], gate, 0.5, next_mod(l, sub))

    shift, scale, _ = mod_rows(0, 0)
    h = _modulate(x, prm["norm_pre"][0, 0], scale, shift)
    for l in range(depth):
        lam_init = 0.8 - 0.6 * math.exp(-0.3 * l)
        x, h = ffn(x, h, l, 0, 0)
        gate = mod_rows(l, 1)[2]
        w_in = wts["w_in"]
        qkv_a = _mm(h, w_in, (l,), 0, 3 * bw, BF16, name="in_proj_na")
        u = _mm(h, w_in, (l,), 3 * bw, bw, F32, name="in_proj_s5")
        qk_rot = _mm_rotary(h, w_in, (l,), 4 * bw, qk_cols, hd)
        vc = _mm(h, w_in, (l,), 4 * bw + qk_cols, bw, BF16, name="in_proj_v")
        gates = _mm(h, w_in, (l,), 5 * bw + qk_cols, 3 * d, BF16, sigmoid=True, name="in_proj_gates")
        y_a = _neighborhood_attention(qkv_a, prm["na_rpb"][l], kw)
        z = _s5_bidirectional(u, prm["s5_tables"][l])
        y_b = _mm_glu(z, wts["s5_w_glu"], (l,), prm["s5_b_glu"][l].reshape(1, -1), False, "s5_glu")
        y_c = _diff_attention(qk_rot, vc, prm["diff_lam"][l], prm["diff_subln"][l], lam_init)
        merged = _merge(y_a, y_b, y_c, wts["w_branch"], l, gates)
        m = _mm(merged, wts["w_out"], (l,), 0, d, F32, name="out_proj")
        x, h = _residual(x, m, prm["norm_post"][l, 1], gate, 1.0, next_mod(l, 1))
        x, h = ffn(x, h, l, 1, 2)
    return x


def kernel(x_prompt, x_sample, c_prompt, c_sample, ada_w, ada_b, norm_pre, norm_post, ffn_w_in, ffn_w_out,
           w_in, w_branch, w_out, na_rpb, s5_lam_re, s5_lam_im, s5_log_dt, s5_b_re, s5_b_im, s5_c_re,
           s5_c_im, s5_d, s5_w_glu, s5_b_glu, diff_lam, diff_subln):
    depth = ada_w.shape[0]
    wts = {"ffn_w_in": ffn_w_in.astype(BF16), "ffn_w_out": ffn_w_out.astype(BF16),
           "w_in": w_in.astype(BF16), "w_branch": w_branch.astype(BF16),
           "w_out": w_out.astype(BF16), "s5_w_glu": s5_w_glu.astype(BF16)}
    prm = {"norm_pre": norm_pre, "norm_post": norm_post, "na_rpb": na_rpb, "s5_b_glu": s5_b_glu,
           "diff_lam": diff_lam, "diff_subln": diff_subln,
           "s5_tables": [_s5_tables(s5_lam_re[l], s5_lam_im[l], s5_log_dt[l], s5_b_re[l], s5_b_im[l],
                                    s5_c_re[l], s5_c_im[l], s5_d[l]) for l in range(depth)]}
    c_all = jnp.concatenate([c_prompt, c_sample], axis=0)
    mod = _ada_mod(c_all, ada_w, ada_b)
    outs = []
    row = 0
    for x in (x_prompt, x_sample):
        ys = []
        for b in range(x.shape[0]):
            ys.append(_encoder_trunk(x[b], mod[:, row], wts, prm))
            row += 1
        outs.append(ys[0][None] if len(ys) == 1 else jnp.stack(ys, axis=0))
    return tuple(outs)
```

```python
import functools
import math

import jax
import jax.numpy as jnp
from jax import lax
from jax.experimental import pallas as pl
from jax.experimental.pallas import tpu as pltpu

F32 = jnp.float32
BF16 = jnp.bfloat16

GRID_W = 64
ROPE_THETA = 10000.0
RMS_EPS = 1e-6
NEG_INF = -1e30
LANES = 128
S5_CHUNK = 16
VMEM_LIMIT_BYTES = 56 * 1024 * 1024
GLU_TM = 2048
FFN_DOWN_TM = 512
FFN_DOWN_TN = 512


def _params(ndims):
    return pltpu.CompilerParams(dimension_semantics=("arbitrary",) * ndims,
                                vmem_limit_bytes=VMEM_LIMIT_BYTES)


def _pick(n, *cands):
    for c in cands:
        if n % c == 0:
            return c
    return n


def _silu(x):
    return x * jax.nn.sigmoid(x)


def _gelu_tanh(x):
    return 0.5 * x * (1.0 + jnp.tanh(math.sqrt(2.0 / math.pi) * (x + 0.044715 * (x * x * x))))


def _ada_kernel(c_ref, w_ref, b_ref, o_ref, *, kc):
    d, nb = c_ref.shape
    tn = w_ref.shape[1]

    def body(k, accs):
        off = pl.multiple_of(k * kc, kc)
        wk = w_ref[pl.ds(off, kc), :]
        ck = _silu(c_ref[pl.ds(off, kc), :])
        return tuple(acc + jnp.sum(wk * ck[:, b:b + 1], axis=0, keepdims=True)
                     for b, acc in enumerate(accs))

    accs = lax.fori_loop(0, d // kc, body, tuple(jnp.zeros((1, tn), F32) for _ in range(nb)))
    for b in range(nb):
        o_ref[b:b + 1, :] = accs[b] + b_ref[...]


def _ada_mod(c_all, ada_w, ada_b):
    nb, d = c_all.shape
    depth, _, n = ada_w.shape
    tn = _pick(n, 512, 256, 128)
    kc = _pick(d, 256, 128, 8)
    return pl.pallas_call(
        functools.partial(_ada_kernel, kc=kc),
        grid=(depth, n // tn),
        in_specs=[pl.BlockSpec((d, nb), lambda l, j: (0, 0)),
                  pl.BlockSpec((None, d, tn), lambda l, j: (l, 0, j)),
                  pl.BlockSpec((None, 1, tn), lambda l, j: (l, 0, j))],
        out_specs=pl.BlockSpec((None, nb, tn), lambda l, j: (l, 0, j)),
        out_shape=jax.ShapeDtypeStruct((depth, nb, n), F32),
        compiler_params=_params(2),
        name="ada_mod",
    )(c_all.T, ada_w, ada_b.reshape(depth, 1, n))


def _modulate_kernel(x_ref, g_ref, sc_ref, sh_ref, o_ref):
    x = x_ref[...]
    y = x * lax.rsqrt(jnp.mean(x * x, axis=-1, keepdims=True) + RMS_EPS) * g_ref[...]
    o_ref[...] = (y * (1.0 + sc_ref[...]) + sh_ref[...]).astype(o_ref.dtype)


def _modulate(x, g, scale, shift):
    m, d = x.shape
    tm = _pick(m, 256, 128, 8)
    row = pl.BlockSpec((1, d), lambda i: (0, 0))
    return pl.pallas_call(
        _modulate_kernel,
        grid=(m // tm,),
        in_specs=[pl.BlockSpec((tm, d), lambda i: (i, 0)), row, row, row],
        out_specs=pl.BlockSpec((tm, d), lambda i: (i, 0)),
        out_shape=jax.ShapeDtypeStruct((m, d), BF16),
        compiler_params=_params(1),
        name="modulate",
    )(x, g.reshape(1, d), scale.reshape(1, d), shift.reshape(1, d))


def _residual_kernel(x_ref, y_ref, g_ref, gate_ref, o_ref, *, coef):
    y = y_ref[...]
    n = y * lax.rsqrt(jnp.mean(y * y, axis=-1, keepdims=True) + RMS_EPS) * g_ref[...]
    o_ref[...] = x_ref[...] + (coef * gate_ref[...]) * n


def _residual_modulate_kernel(x_ref, y_ref, g_ref, gate_ref, g2_ref, sc_ref, sh_ref, o_ref, h_ref, *, coef):
    y = y_ref[...]
    n = y * lax.rsqrt(jnp.mean(y * y, axis=-1, keepdims=True) + RMS_EPS) * g_ref[...]
    x = x_ref[...] + (coef * gate_ref[...]) * n
    o_ref[...] = x
    hn = x * lax.rsqrt(jnp.mean(x * x, axis=-1, keepdims=True) + RMS_EPS) * g2_ref[...]
    h_ref[...] = (hn * (1.0 + sc_ref[...]) + sh_ref[...]).astype(h_ref.dtype)


def _residual(x, y, g, gate, coef, next_mod=None):
    m, d = x.shape
    tm = _pick(m, 256, 128, 8)
    row = pl.BlockSpec((1, d), lambda i: (0, 0))
    blk = pl.BlockSpec((tm, d), lambda i: (i, 0))
    if next_mod is None:
        return pl.pallas_call(
            functools.partial(_residual_kernel, coef=coef),
            grid=(m // tm,),
            in_specs=[blk, blk, row, row],
            out_specs=blk,
            out_shape=jax.ShapeDtypeStruct((m, d), F32),
            compiler_params=_params(1),
            name="residual",
        )(x, y, g.reshape(1, d), gate.reshape(1, d)), None
    return pl.pallas_call(
        functools.partial(_residual_modulate_kernel, coef=coef),
        grid=(m // tm,),
        in_specs=[blk, blk, row, row, row, row, row],
        out_specs=(blk, blk),
        out_shape=(jax.ShapeDtypeStruct((m, d), F32), jax.ShapeDtypeStruct((m, d), BF16)),
        compiler_params=_params(1),
        name="residual_modulate",
    )(x, y, g.reshape(1, d), gate.reshape(1, d), *[t.reshape(1, d) for t in next_mod])


def _mm_kernel(a_ref, w_ref, o_ref, *, sigmoid):
    y = jnp.dot(a_ref[...], w_ref[...], preferred_element_type=F32)
    o_ref[...] = (jax.nn.sigmoid(y) if sigmoid else y).astype(o_ref.dtype)


def _mm(a, w, lead, col_off, ncols, out_dtype, tm_pref=1024, tn_pref=1024, sigmoid=False, name="mm"):
    m, k = a.shape
    tm = _pick(m, *[c for c in (1024, 512, 256, 128, 8) if c <= tm_pref])
    tn = _pick(math.gcd(col_off, ncols), *[c for c in (1024, 768, 512, 384, 256, 128) if c <= tn_pref])
    off = col_off // tn
    w_spec = pl.BlockSpec((None,) * len(lead) + (k, tn), lambda i, j: tuple(lead) + (0, off + j))
    return pl.pallas_call(
        functools.partial(_mm_kernel, sigmoid=sigmoid),
        grid=(m // tm, ncols // tn),
        in_specs=[pl.BlockSpec((tm, k), lambda i, j: (i, 0)), w_spec],
        out_specs=pl.BlockSpec((tm, tn), lambda i, j: (i, j)),
        out_shape=jax.ShapeDtypeStruct((m, ncols), out_dtype),
        compiler_params=_params(2),
        name=name,
    )(a, w)


def _mm_glu_kernel(a_ref, w1_ref, w2_ref, b1_ref, b2_ref, o_ref, *, swiglu):
    a = a_ref[...].astype(BF16)
    y1 = jnp.dot(a, w1_ref[...], preferred_element_type=F32) + b1_ref[...]
    y2 = jnp.dot(a, w2_ref[...], preferred_element_type=F32) + b2_ref[...]
    out = _silu(y1) * y2 if swiglu else y1 * jax.nn.sigmoid(y2)
    o_ref[...] = out.astype(o_ref.dtype)


def _mm_glu(a, w, lead, bias, swiglu, name):
    m, k = a.shape
    half = w.shape[-1] // 2
    tm = _pick(m, *[c for c in (2048, 1024, 512, 256, 128, 8) if c <= GLU_TM])
    tn = _pick(half, 512, 256, 128)
    nj = half // tn
    nl = len(lead)
    w1 = pl.BlockSpec((None,) * nl + (k, tn), lambda i, j: tuple(lead) + (0, j))
    w2 = pl.BlockSpec((None,) * nl + (k, tn), lambda i, j: tuple(lead) + (0, nj + j))
    b1 = pl.BlockSpec((1, tn), lambda i, j: (0, j))
    b2 = pl.BlockSpec((1, tn), lambda i, j: (0, nj + j))
    return pl.pallas_call(
        functools.partial(_mm_glu_kernel, swiglu=swiglu),
        grid=(m // tm, nj),
        in_specs=[pl.BlockSpec((tm, k), lambda i, j: (i, 0)), w1, w2, b1, b2],
        out_specs=pl.BlockSpec((tm, tn), lambda i, j: (i, j)),
        out_shape=jax.ShapeDtypeStruct((m, half), BF16),
        compiler_params=_params(2),
        name=name,
    )(a, w, w, bias, bias)


def _merge_kernel(ya_ref, yb_ref, yc_ref, w_ref, ga_ref, gb_ref, gc_ref, o_ref):
    acc = ga_ref[...].astype(F32) * jnp.dot(ya_ref[...], w_ref[0], preferred_element_type=F32)
    acc += gb_ref[...].astype(F32) * jnp.dot(yb_ref[...], w_ref[1], preferred_element_type=F32)
    acc += gc_ref[...].astype(F32) * jnp.dot(yc_ref[...], w_ref[2], preferred_element_type=F32)
    o_ref[...] = acc.astype(o_ref.dtype)


def _merge(ya, yb, yc, w_branch, l, gates):
    m, bw = ya.shape
    d = w_branch.shape[-1]
    tm = _pick(m, 512, 256, 128, 8)
    tn = _pick(d, 512, 256, 128)
    nj = d // tn
    y_spec = pl.BlockSpec((tm, bw), lambda i, j: (i, 0))
    g_specs = [pl.BlockSpec((tm, tn), functools.partial(lambda i, j, k: (i, k * nj + j), k=k))
               for k in range(3)]
    return pl.pallas_call(
        _merge_kernel,
        grid=(m // tm, nj),
        in_specs=[y_spec, y_spec, y_spec,
                  pl.BlockSpec((None, 3, bw, tn), lambda i, j: (l, 0, 0, j))] + g_specs,
        out_specs=pl.BlockSpec((tm, tn), lambda i, j: (i, j)),
        out_shape=jax.ShapeDtypeStruct((m, d), BF16),
        compiler_params=_params(2),
        name="branch_merge",
    )(ya, yb, yc, w_branch, gates, gates, gates)


def _na_bias_kernel(rpb_ref, o_ref, *, kw, nr, nc):
    h = pl.program_id(0)
    w = GRID_W
    q = lax.broadcasted_iota(jnp.int32, (w, 2 * w), 0)
    c2 = lax.broadcasted_iota(jnp.int32, (w, 2 * w), 1)
    second = c2 >= w
    c = jnp.where(second, c2 - w, c2)
    dc = jnp.clip(c - q, -(kw - 1), kw - 1) + (kw - 1)
    cs = jnp.clip(q - kw // 2, 0, w - kw)
    ok = jnp.logical_and(c >= cs, c < cs + kw)
    for dr in range(nr - 1):
        t = jnp.zeros((w, 2 * w), F32)
        for dd in range(nc):
            lo = rpb_ref[(h * nr + dr) * nc + dd]
            hi = rpb_ref[(h * nr + dr + 1) * nc + dd]
            t = jnp.where(dc == dd, jnp.where(second, hi, lo), t)
        o_ref[dr] = jnp.where(ok, t, NEG_INF)


def _na_bias(rpb, kw):
    heads, nr, nc = rpb.shape
    return pl.pallas_call(
        functools.partial(_na_bias_kernel, kw=kw, nr=nr, nc=nc),
        grid=(heads,),
        in_specs=[pl.BlockSpec(memory_space=pltpu.SMEM)],
        out_specs=pl.BlockSpec((None, nr - 1, GRID_W, 2 * GRID_W), lambda h: (h, 0, 0, 0)),
        out_shape=jax.ShapeDtypeStruct((heads, nr - 1, GRID_W, 2 * GRID_W), F32),
        compiler_params=_params(1),
        name="na_bias",
    )(rpb.reshape(-1))


def _na_kernel(q_ref, k_ref, v_ref, tab_ref, o_ref, *, rb, rows, kh, kh_full, scale):
    i = pl.program_id(1)
    w = GRID_W
    scores, toks = [], []
    for rr in range(rb):
        r = i * rb + rr
        start = jnp.clip(r - kh // 2, 0, rows - kh)
        dr0 = start - r + (kh_full - 1)
        tok = pl.multiple_of(start * w, w)
        q = q_ref[rr * w:(rr + 1) * w, :]
        kwin = k_ref[pl.ds(tok, kh * w), :]
        s = lax.dot_general(q, kwin, (((1,), (1,)), ((), ())), preferred_element_type=F32) * scale
        bias = jnp.concatenate([tab_ref[dr0 + 2 * mm] for mm in range(kh // 2)], axis=1)
        scores.append(s + bias)
        toks.append(tok)
    for rr in range(rb):
        s = scores[rr]
        p = jnp.exp(s - jnp.max(s, axis=-1, keepdims=True))
        denom = jnp.sum(p, axis=-1, keepdims=True)
        o = jnp.dot(p.astype(BF16), v_ref[pl.ds(toks[rr], kh * w), :], preferred_element_type=F32)
        o_ref[rr * w:(rr + 1) * w, :] = (o / denom).astype(o_ref.dtype)


def _neighborhood_attention(qkv, rpb, kw):
    L = qkv.shape[0]
    heads, nr, _ = rpb.shape
    bw = qkv.shape[1] // 3
    hd = bw // heads
    assert hd == LANES, "neighbourhood-attention head dim must fill one lane tile"
    rows = L // GRID_W
    kh_full = (nr + 1) // 2
    kh = min(kh_full, rows)
    assert kh % 2 == 0
    tab = _na_bias(rpb, kw)
    rb = _pick(rows, 8, 4, 2, 1)
    return pl.pallas_call(
        functools.partial(_na_kernel, rb=rb, rows=rows, kh=kh, kh_full=kh_full, scale=hd ** -0.5),
        grid=(heads, rows // rb),
        in_specs=[pl.BlockSpec((rb * GRID_W, hd), lambda h, i: (i, h)),
                  pl.BlockSpec((L, hd), lambda h, i: (0, heads + h)),
                  pl.BlockSpec((L, hd), lambda h, i: (0, 2 * heads + h)),
                  pl.BlockSpec((None, nr - 1, GRID_W, 2 * GRID_W), lambda h, i: (h, 0, 0, 0))],
        out_specs=pl.BlockSpec((rb * GRID_W, hd), lambda h, i: (i, h)),
        out_shape=jax.ShapeDtypeStruct((L, bw), BF16),
        compiler_params=_params(2),
        name="neighborhood_attention",
    )(qkv, qkv, qkv, tab)


def _mm_rotary_kernel(a_ref, w_ref, cos_ref, sin_ref, o_ref, *, hd, n_q_tiles, q_scale):
    y = jnp.dot(a_ref[...], w_ref[...], preferred_element_type=F32)
    scale = jnp.where(pl.program_id(1) < n_q_tiles, q_scale, 1.0)
    cos, sin = cos_ref[...] * scale, sin_ref[...] * scale
    lane = lax.broadcasted_iota(jnp.int32, cos.shape, 1)
    first = (lane % hd) < (hd // 2)
    for t in range(y.shape[1] // LANES):
        x = y[:, t * LANES:(t + 1) * LANES]
        rot = jnp.where(first, pltpu.roll(x, LANES - hd // 2, 1), pltpu.roll(x, hd // 2, 1))
        o_ref[:, t * LANES:(t + 1) * LANES] = (x * cos + rot * sin).astype(o_ref.dtype)


def _mm_rotary(a, w, lead, col_off, ncols, hd):
    L, k = a.shape
    assert LANES % hd == 0
    half = hd // 2
    inv_freq = ROPE_THETA ** (-jnp.arange(half, dtype=F32) / half)
    ang = jnp.arange(L, dtype=F32)[:, None] * inv_freq[None, :]
    cos, sin = jnp.cos(ang), jnp.sin(ang)
    reps = LANES // hd
    cos_t = jnp.tile(jnp.concatenate([cos, cos], axis=1), (1, reps))
    sin_t = jnp.tile(jnp.concatenate([-sin, sin], axis=1), (1, reps))
    tm = _pick(L, 1024, 512, 256, 128, 8)
    tn = _pick(math.gcd(col_off, ncols // 2), 768, 512, 384, 256, 128)
    off = col_off // tn
    tab = pl.BlockSpec((tm, LANES), lambda i, j: (i, 0))
    return pl.pallas_call(
        functools.partial(_mm_rotary_kernel, hd=hd, n_q_tiles=ncols // 2 // tn,
                          q_scale=hd ** -0.5 * math.log2(math.e)),
        grid=(L // tm, ncols // tn),
        in_specs=[pl.BlockSpec((tm, k), lambda i, j: (i, 0)),
                  pl.BlockSpec((None,) * len(lead) + (k, tn), lambda i, j: tuple(lead) + (0, off + j)),
                  tab, tab],
        out_specs=pl.BlockSpec((tm, tn), lambda i, j: (i, j)),
        out_shape=jax.ShapeDtypeStruct((L, ncols), BF16),
        compiler_params=_params(2),
        name="in_proj_qk_rotary",
    )(a, w, cos_t, sin_t)


DIFF_TQ = 256
DIFF_TK = 8192
DIFF_SUB = 256
DIFF_LOOKAHEAD = 5
DIFF_ONES_ROWS = 16


def _diff_attn_kernel(q1_ref, q2_ref, k1_ref, k2_ref, v_ref, lam_ref, g_ref, o_ref,
                      qp_sc, acc1, acc2, m1, m2, *, tk, sub, hd, vd, lam_init):
    tq = q1_ref.shape[0]
    nk = v_ref.shape[0]
    par = pl.program_id(0) % (LANES // hd)
    rowi = lax.broadcasted_iota(jnp.int32, (LANES, tq), 0)
    keep = jnp.logical_and(rowi >= par * hd, rowi < (par + 1) * hd)
    for t, q_ref in enumerate((q1_ref, q2_ref)):
        qp_sc[t] = jnp.where(keep, q_ref[...].astype(F32).T, 0.0).astype(BF16)
    for acc, m in ((acc1, m1), (acc2, m2)):
        acc[...] = jnp.zeros(acc.shape, F32)
        m[...] = jnp.full(m.shape, NEG_INF, F32)

    def body(c, carry):
        off = pl.multiple_of(c * tk, tk)
        tasks = [(j, t) for j in range(tk // sub) for t in range(2)]
        k_refs, accs, ms = (k1_ref, k2_ref), (acc1, acc2), (m1, m2)

        def scores(j, t):
            ks = k_refs[t][pl.ds(off + j * sub, sub), :]
            return jnp.dot(ks, qp_sc[t], preferred_element_type=F32)

        pending = [scores(*task) for task in tasks[:DIFF_LOOKAHEAD]]
        for idx, (j, t) in enumerate(tasks):
            s = pending.pop(0)
            if idx + DIFF_LOOKAHEAD < len(tasks):
                pending.append(scores(*tasks[idx + DIFF_LOOKAHEAD]))
            m_old = ms[t][...]
            m_new = jnp.maximum(m_old, jnp.max(s, axis=0, keepdims=True))
            p = jnp.exp2(s - m_new).astype(BF16)
            pv = jnp.dot(v_ref[c, :, j * sub:(j + 1) * sub], p, preferred_element_type=F32)
            accs[t][...] = jnp.exp2(m_old - m_new) * accs[t][...] + pv
            ms[t][...] = m_new
        return carry

    lax.fori_loop(0, nk, body, 0)
    lp = lam_ref[...]
    lam = (jnp.exp(jnp.sum(lp[0:1] * lp[1:2], axis=-1, keepdims=True))
           - jnp.exp(jnp.sum(lp[2:3] * lp[3:4], axis=-1, keepdims=True)) + lam_init)
    a1, a2 = acc1[...], acc2[...]
    o = a1[:vd] / a1[vd:vd + 1] - lam * (a2[:vd] / a2[vd:vd + 1])
    o = o * lax.rsqrt(jnp.mean(o * o, axis=0, keepdims=True) + RMS_EPS) * g_ref[...]
    o_ref[...] = (o * (1.0 - lam_init)).astype(o_ref.dtype)


def _diff_attention(qk_rot, v, diff_lam, subln_g, lam_init):
    L = qk_rot.shape[0]
    hd = diff_lam.shape[-1]
    heads = qk_rot.shape[1] // (4 * hd)
    vd = v.shape[1] // heads
    assert vd == LANES and LANES % hd == 0 and (heads * hd) % LANES == 0
    hp = LANES // hd
    nb = heads * hd // LANES
    tq = _pick(L, DIFF_TQ, 128)
    tk = _pick(L, DIFF_TK, 4096, 2048, 1024, 512, 256, 128)
    nk = L // tk
    vt = jnp.transpose(v.reshape(nk, tk, heads, vd), (2, 0, 3, 1))
    vt = jnp.concatenate([vt, jnp.ones((heads, nk, DIFF_ONES_ROWS, tk), BF16)], axis=2)
    acc = pltpu.VMEM((vd + DIFF_ONES_ROWS, tq), F32)
    stat = pltpu.VMEM((1, tq), F32)
    out_t = pl.pallas_call(
        functools.partial(_diff_attn_kernel, tk=tk, sub=_pick(tk, DIFF_SUB), hd=hd, vd=vd, lam_init=lam_init),
        grid=(heads, L // tq),
        in_specs=[pl.BlockSpec((tq, LANES), lambda h, i: (i, h // hp)),
                  pl.BlockSpec((tq, LANES), lambda h, i: (i, nb + h // hp)),
                  pl.BlockSpec((L, LANES), lambda h, i: (0, 2 * nb + h // hp)),
                  pl.BlockSpec((L, LANES), lambda h, i: (0, 3 * nb + h // hp)),
                  pl.BlockSpec((None, nk, vd + DIFF_ONES_ROWS, tk), lambda h, i: (h, 0, 0, 0)),
                  pl.BlockSpec(diff_lam.shape, lambda h, i: (0, 0)),
                  pl.BlockSpec((vd, 1), lambda h, i: (0, 0))],
        out_specs=pl.BlockSpec((None, vd, tq), lambda h, i: (h, 0, i)),
        out_shape=jax.ShapeDtypeStruct((heads, vd, L), BF16),
        scratch_shapes=[pltpu.VMEM((2, LANES, tq), BF16), acc, acc, stat, stat],
        compiler_params=_params(2),
        name="diff_attention",
    )(qk_rot, qk_rot, qk_rot, qk_rot, vt, diff_lam, subln_g.reshape(vd, 1))
    return jnp.transpose(out_t, (2, 0, 1)).reshape(L, heads * vd)


def _s5_tables(lam_re, lam_im, log_dt, b_re, b_im, c_re, c_im, d_skip):
    T = S5_CHUNK
    _, G, P, HG = b_re.shape
    dt = jnp.exp(log_dt.astype(F32))[..., None]
    lr, li = lam_re.astype(F32), lam_im.astype(F32)
    j = jnp.arange(T + 1, dtype=F32)[:, None, None, None]
    mag = jnp.exp(lr * dt * j)
    pw_re, pw_im = mag * jnp.cos(li * dt * j), mag * jnp.sin(li * dt * j)
    den = lr * lr + li * li
    nr_, ni_ = pw_re[1] - 1.0, pw_im[1]
    f_re, f_im = (nr_ * lr + ni_ * li) / den, (ni_ * lr - nr_ * li) / den
    bb_re = f_re[..., None] * b_re - f_im[..., None] * b_im
    bb_im = f_re[..., None] * b_im + f_im[..., None] * b_re
    cr, ci = c_re.astype(F32)[None], c_im.astype(F32)[None]
    cp_re = cr * pw_re[:, :, :, None, :] - ci * pw_im[:, :, :, None, :]
    cp_im = cr * pw_im[:, :, :, None, :] + ci * pw_re[:, :, :, None, :]
    hp = lax.Precision.HIGHEST
    kj = (jnp.einsum('jdghp,dgpk->jdghk', cp_re[:T], bb_re, precision=hp)
          - jnp.einsum('jdghp,dgpk->jdghk', cp_im[:T], bb_im, precision=hp))
    sig = jnp.arange(T)[:, None]
    tau = jnp.arange(T)[None, :]
    lag_f = jnp.clip(tau - sig, 0, T - 1)
    lag_b = jnp.clip(sig - tau, 0, T - 1)
    kf = jnp.where((tau >= sig)[:, :, None, None, None], kj[lag_f, 0], 0.0)
    kb = jnp.where((sig >= tau)[:, :, None, None, None], kj[lag_b, 1], 0.0)
    kin = jnp.transpose(kf + kb, (2, 0, 4, 1, 3)).reshape(G, T * HG, T * HG)
    def bend_dir(d, order):
        pr, pi = pw_re[order, d], pw_im[order, d]
        re = pr[..., None] * bb_re[d][None] - pi[..., None] * bb_im[d][None]
        im = pr[..., None] * bb_im[d][None] + pi[..., None] * bb_re[d][None]
        to_rows = lambda t: jnp.transpose(t, (1, 0, 3, 2)).reshape(G, T * HG, P)
        return to_rows(re), to_rows(im)
    bf_re, bf_im = bend_dir(0, jnp.arange(T - 1, -1, -1))
    bb_re_, bb_im_ = bend_dir(1, jnp.arange(T))
    bend = jnp.stack([bf_re, bf_im, bb_re_, bb_im_], axis=2)
    def wout_dir(d, order):
        to_cols = lambda t: jnp.transpose(t, (1, 3, 0, 2)).reshape(G, P, T * HG)
        return to_cols(cp_re[order, d]), to_cols(-cp_im[order, d])
    wf_re, wf_im = wout_dir(0, jnp.arange(1, T + 1))
    wb_re, wb_im = wout_dir(1, jnp.arange(T, 0, -1))
    wout = jnp.stack([wf_re, wf_im, wb_re, wb_im], axis=1)
    eye2 = jnp.eye(2, dtype=F32)
    G2 = G // 2
    kin2 = jnp.einsum('karc,ab->karbc', kin.reshape(G2, 2, T * HG, T * HG), eye2)
    kin2 = kin2.reshape(G2, 2 * T * HG, 2 * T * HG)
    bend2 = jnp.einsum('karqp,ab->karqbp', bend.reshape(G2, 2, T * HG, 4, P), eye2)
    bend2 = bend2.reshape(G2, 2 * T * HG, 4 * 2 * P)
    wout2 = jnp.einsum('kaqpc,ab->kqapbc', wout.reshape(G2, 2, 4, P, T * HG), eye2)
    wout2 = wout2.reshape(G2, 4 * 2 * P, 2 * T * HG)
    a = jnp.stack([pw_re[T, 0], pw_im[T, 0], pw_re[T, 1], pw_im[T, 1]], axis=1)
    a2 = jnp.transpose(a.reshape(G2, 2, 4, P), (0, 2, 1, 3)).reshape(G2, 4, 2 * P)
    dsk = jnp.tile(d_skip.astype(F32)[:, None, :], (1, T, 1)).reshape(G2, 1, 2 * T * HG)
    return kin2.astype(BF16), bend2.astype(BF16), wout2.astype(BF16), a2, dsk


def _s5_kernel(u_ref, kin_ref, bend_ref, wout_ref, a_ref, d_ref, pout_ref, o_ref, z_sc, s_sc, g_sc):
    nc = u_ref.shape[0]
    pb, cw = kin_ref.shape[0], kin_ref.shape[1]
    sw = a_ref.shape[2]
    sub = 8
    nt = nc // sub
    row = lax.broadcasted_iota(jnp.int32, (sub, sw), 0)
    gw = cw // 2
    ngrp = 2 * pb
    nth = gw // LANES
    tw = ngrp * LANES
    for pp in range(pb):
        u = u_ref[:, pp * cw:(pp + 1) * cw]
        ub = u.astype(BF16)
        z_sc[...] = jnp.dot(ub, bend_ref[pp], preferred_element_type=F32)
        afr, afi, abr, abi = (jnp.broadcast_to(a_ref[pp, t:t + 1, :], (sub, sw)) for t in range(4))

        def body(i, carry):
            xfr, xfi, xbr, xbi = carry
            rf = pl.multiple_of(i * sub, sub)
            rb = pl.multiple_of((nt - 1 - i) * sub, sub)
            zf = z_sc[pl.ds(rf, sub), 0:2 * sw]
            zb = z_sc[pl.ds(rb, sub), 2 * sw:4 * sw]
            sfr = sfi = sbr = sbi = jnp.zeros((sub, sw), F32)
            for jf in range(sub):
                jb = sub - 1 - jf
                sfr = jnp.where(row == jf, xfr, sfr)
                sfi = jnp.where(row == jf, xfi, sfi)
                sbr = jnp.where(row == jb, xbr, sbr)
                sbi = jnp.where(row == jb, xbi, sbi)
                zfj = jnp.broadcast_to(zf[jf:jf + 1, :], (sub, 2 * sw))
                zbj = jnp.broadcast_to(zb[jb:jb + 1, :], (sub, 2 * sw))
                xfr, xfi = (afr * xfr - afi * xfi + zfj[:, 0:sw], afr * xfi + afi * xfr + zfj[:, sw:2 * sw])
                xbr, xbi = (abr * xbr - abi * xbi + zbj[:, 0:sw], abr * xbi + abi * xbr + zbj[:, sw:2 * sw])
            s_sc[pl.ds(rf, sub), 0:sw] = sfr
            s_sc[pl.ds(rf, sub), sw:2 * sw] = sfi
            s_sc[pl.ds(rb, sub), 2 * sw:3 * sw] = sbr
            s_sc[pl.ds(rb, sub), 3 * sw:4 * sw] = sbi
            return xfr, xfi, xbr, xbi

        zero = jnp.zeros((sub, sw), F32)
        lax.fori_loop(0, nt, body, (zero, zero, zero, zero))
        y = jnp.dot(ub, kin_ref[pp], preferred_element_type=F32)
        y += jnp.dot(s_sc[...].astype(BF16), wout_ref[pp], preferred_element_type=F32)
        y += d_ref[pp] * u
        g_sc[:, pp * cw:(pp + 1) * cw] = _gelu_tanh(y).astype(BF16)
    for th in range(nth):
        src = jnp.concatenate([g_sc[:, g * gw + th * LANES:g * gw + (th + 1) * LANES] for g in range(ngrp)],
                              axis=1)
        res = jnp.dot(src, pout_ref[...], preferred_element_type=F32)
        o_ref[:, th * tw:(th + 1) * tw] = res.astype(o_ref.dtype)


def _s5_bidirectional(u, tables, l):
    kin, bend, wout, a, dsk = tables
    L, bw = u.shape
    T = S5_CHUNK
    nc = L // T
    g2 = kin.shape[1]
    cw = kin.shape[2]
    hg = cw // (2 * T)
    pb = LANES // (2 * hg)
    assert L % T == 0 and nc % 8 == 0 and g2 * 2 * hg == bw and g2 % pb == 0 and (T * hg) % LANES == 0
    uc = jnp.transpose(u.reshape(nc, T, g2, 2, hg), (0, 2, 3, 1, 4)).reshape(nc, g2 * cw)
    sw4 = bend.shape[3]
    ngrp = 2 * pb
    r = jnp.arange(ngrp * LANES)
    dst = ((r % LANES) // hg) * (ngrp * hg) + (r // LANES) * hg + r % hg
    perm_out = jax.nn.one_hot(dst, ngrp * LANES, dtype=BF16)
    nblk = g2 // pb
    z = pl.pallas_call(
        _s5_kernel,
        grid=(nblk,),
        in_specs=[pl.BlockSpec((nc, pb * cw), lambda g: (0, g)),
                  pl.BlockSpec((None, pb, cw, cw), lambda g: (l, g, 0, 0)),
                  pl.BlockSpec((None, pb, cw, sw4), lambda g: (l, g, 0, 0)),
                  pl.BlockSpec((None, pb, sw4, cw), lambda g: (l, g, 0, 0)),
                  pl.BlockSpec((None, pb, 4, sw4 // 4), lambda g: (l, g, 0, 0)),
                  pl.BlockSpec((None, pb, 1, cw), lambda g: (l, g, 0, 0)),
                  pl.BlockSpec((ngrp * LANES, ngrp * LANES), lambda g: (0, 0))],
        out_specs=pl.BlockSpec((nc, pb * cw), lambda g: (0, g)),
        out_shape=jax.ShapeDtypeStruct((nc, g2 * cw), BF16),
        scratch_shapes=[pltpu.VMEM((nc, sw4), F32), pltpu.VMEM((nc, sw4), F32),
                        pltpu.VMEM((nc, pb * cw), BF16)],
        compiler_params=_params(1),
        name="s5_scan",
    )(uc, kin, bend, wout, a, dsk, perm_out)
    return jnp.transpose(z.reshape(nc, nblk, T, LANES), (0, 2, 1, 3)).reshape(nc * T, bw)


def _encoder_trunk(x, mod, wts, prm):
    L, d = x.shape
    depth = wts["ffn_w_in"].shape[0]
    bw = wts["w_branch"].shape[2]
    hd = prm["diff_lam"].shape[-1]
    kw = (prm["na_rpb"].shape[-1] + 1) // 2
    qk_cols = wts["w_in"].shape[-1] - 5 * bw - 3 * d
    zero_bias = jnp.zeros((1, wts["ffn_w_in"].shape[-1]), F32)

    def mod_rows(l, sub):
        return tuple(mod[l, (3 * sub + t) * d:(3 * sub + t + 1) * d] for t in range(3))

    def next_mod(l, sub):
        l, sub = (l, sub + 1) if sub < 2 else (l + 1, 0)
        if l == depth:
            return None
        shift, scale, _ = mod_rows(l, sub)
        return prm["norm_pre"][l, sub], scale, shift

    def ffn(x, h, l, which, sub):
        gate = mod_rows(l, sub)[2]
        act = _mm_glu(h, wts["ffn_w_in"], (l, which), zero_bias, True, "ffn_up")
        y = _mm(act, wts["ffn_w_out"], (l, which), 0, d, F32, tm_pref=FFN_DOWN_TM, tn_pref=FFN_DOWN_TN,
                name="ffn_down")
        return _residual(x, y, prm["norm_post"][l, sub], gate, 0.5, next_mod(l, sub))

    shift, scale, _ = mod_rows(0, 0)
    h = _modulate(x, prm["norm_pre"][0, 0], scale, shift)
    for l in range(depth):
        lam_init = 0.8 - 0.6 * math.exp(-0.3 * l)
        x, h = ffn(x, h, l, 0, 0)
        gate = mod_rows(l, 1)[2]
        w_in = wts["w_in"]
        qkv_a = _mm(h, w_in, (l,), 0, 3 * bw, BF16, name="in_proj_na")
        u = _mm(h, w_in, (l,), 3 * bw, bw, F32, name="in_proj_s5")
        qk_rot = _mm_rotary(h, w_in, (l,), 4 * bw, qk_cols, hd)
        vc = _mm(h, w_in, (l,), 4 * bw + qk_cols, bw, BF16, name="in_proj_v")
        gates = _mm(h, w_in, (l,), 5 * bw + qk_cols, 3 * d, BF16, sigmoid=True, name="in_proj_gates")
        y_a = _neighborhood_attention(qkv_a, prm["na_rpb"][l], kw)
        z = _s5_bidirectional(u, prm["s5_tables"], l)
        y_b = _mm_glu(z, wts["s5_w_glu"], (l,), prm["s5_b_glu"][l].reshape(1, -1), False, "s5_glu")
        y_c = _diff_attention(qk_rot, vc, prm["diff_lam"][l], prm["diff_subln"][l], lam_init)
        merged = _merge(y_a, y_b, y_c, wts["w_branch"], l, gates)
        m = _mm(merged, wts["w_out"], (l,), 0, d, F32, name="out_proj")
        x, h = _residual(x, m, prm["norm_post"][l, 1], gate, 1.0, next_mod(l, 1))
        x, h = ffn(x, h, l, 1, 2)
    return x


def kernel(x_prompt, x_sample, c_prompt, c_sample, ada_w, ada_b, norm_pre, norm_post, ffn_w_in, ffn_w_out,
           w_in, w_branch, w_out, na_rpb, s5_lam_re, s5_lam_im, s5_log_dt, s5_b_re, s5_b_im, s5_c_re,
           s5_c_im, s5_d, s5_w_glu, s5_b_glu, diff_lam, diff_subln):
    depth = ada_w.shape[0]
    wts = {"ffn_w_in": ffn_w_in.astype(BF16), "ffn_w_out": ffn_w_out.astype(BF16),
           "w_in": w_in.astype(BF16), "w_branch": w_branch.astype(BF16),
           "w_out": w_out.astype(BF16), "s5_w_glu": s5_w_glu.astype(BF16)}
    prm = {"norm_pre": norm_pre, "norm_post": norm_post, "na_rpb": na_rpb, "s5_b_glu": s5_b_glu,
           "diff_lam": diff_lam, "diff_subln": diff_subln,
           "s5_tables": jax.vmap(_s5_tables)(s5_lam_re, s5_lam_im, s5_log_dt, s5_b_re, s5_b_im,
                                             s5_c_re, s5_c_im, s5_d)}
    c_all = jnp.concatenate([c_prompt, c_sample], axis=0)
    mod = _ada_mod(c_all, ada_w, ada_b)
    outs = []
    row = 0
    for x in (x_prompt, x_sample):
        ys = []
        for b in range(x.shape[0]):
            ys.append(_encoder_trunk(x[b], mod[:, row], wts, prm))
            row += 1
        outs.append(ys[0][None] if len(ys) == 1 else jnp.stack(ys, axis=0))
    return tuple(outs)
```

```python
import functools
import math

import jax
import jax.numpy as jnp
from jax import lax
from jax.experimental import pallas as pl
from jax.experimental.pallas import tpu as pltpu

F32 = jnp.float32
BF16 = jnp.bfloat16

GRID_W = 64
ROPE_THETA = 10000.0
RMS_EPS = 1e-6
NEG_INF = -1e30
LANES = 128
S5_CHUNK = 16
VMEM_LIMIT_BYTES = 56 * 1024 * 1024
GLU_TM = 2048
FFN_DOWN_TM = 512
FFN_DOWN_TN = 512


def _params(ndims):
    return pltpu.CompilerParams(dimension_semantics=("arbitrary",) * ndims,
                                vmem_limit_bytes=VMEM_LIMIT_BYTES)


def _pick(n, *cands):
    for c in cands:
        if n % c == 0:
            return c
    return n


def _silu(x):
    return x * jax.nn.sigmoid(x)


def _gelu_tanh(x):
    return 0.5 * x * (1.0 + jnp.tanh(math.sqrt(2.0 / math.pi) * (x + 0.044715 * (x * x * x))))


def _ada_kernel(c_ref, w_ref, b_ref, o_ref, *, kc):
    d, nb = c_ref.shape
    tn = w_ref.shape[1]

    def body(k, accs):
        off = pl.multiple_of(k * kc, kc)
        wk = w_ref[pl.ds(off, kc), :]
        ck = _silu(c_ref[pl.ds(off, kc), :])
        return tuple(acc + jnp.sum(wk * ck[:, b:b + 1], axis=0, keepdims=True)
                     for b, acc in enumerate(accs))

    accs = lax.fori_loop(0, d // kc, body, tuple(jnp.zeros((1, tn), F32) for _ in range(nb)))
    for b in range(nb):
        o_ref[b:b + 1, :] = accs[b] + b_ref[...]


def _ada_mod(c_all, ada_w, ada_b):
    nb, d = c_all.shape
    depth, _, n = ada_w.shape
    tn = _pick(n, 512, 256, 128)
    kc = _pick(d, 256, 128, 8)
    return pl.pallas_call(
        functools.partial(_ada_kernel, kc=kc),
        grid=(depth, n // tn),
        in_specs=[pl.BlockSpec((d, nb), lambda l, j: (0, 0)),
                  pl.BlockSpec((None, d, tn), lambda l, j: (l, 0, j)),
                  pl.BlockSpec((None, 1, tn), lambda l, j: (l, 0, j))],
        out_specs=pl.BlockSpec((None, nb, tn), lambda l, j: (l, 0, j)),
        out_shape=jax.ShapeDtypeStruct((depth, nb, n), F32),
        compiler_params=_params(2),
        name="ada_mod",
    )(c_all.T, ada_w, ada_b.reshape(depth, 1, n))


def _modulate_kernel(x_ref, g_ref, sc_ref, sh_ref, o_ref):
    x = x_ref[...]
    y = x * lax.rsqrt(jnp.mean(x * x, axis=-1, keepdims=True) + RMS_EPS) * g_ref[...]
    o_ref[...] = (y * (1.0 + sc_ref[...]) + sh_ref[...]).astype(o_ref.dtype)


def _modulate(x, g, scale, shift):
    m, d = x.shape
    tm = _pick(m, 256, 128, 8)
    row = pl.BlockSpec((1, d), lambda i: (0, 0))
    return pl.pallas_call(
        _modulate_kernel,
        grid=(m // tm,),
        in_specs=[pl.BlockSpec((tm, d), lambda i: (i, 0)), row, row, row],
        out_specs=pl.BlockSpec((tm, d), lambda i: (i, 0)),
        out_shape=jax.ShapeDtypeStruct((m, d), BF16),
        compiler_params=_params(1),
        name="modulate",
    )(x, g.reshape(1, d), scale.reshape(1, d), shift.reshape(1, d))


def _residual_kernel(x_ref, y_ref, g_ref, gate_ref, o_ref, *, coef):
    y = y_ref[...]
    n = y * lax.rsqrt(jnp.mean(y * y, axis=-1, keepdims=True) + RMS_EPS) * g_ref[...]
    o_ref[...] = x_ref[...] + (coef * gate_ref[...]) * n


def _residual_modulate_kernel(x_ref, y_ref, g_ref, gate_ref, g2_ref, sc_ref, sh_ref, o_ref, h_ref, *, coef):
    y = y_ref[...]
    n = y * lax.rsqrt(jnp.mean(y * y, axis=-1, keepdims=True) + RMS_EPS) * g_ref[...]
    x = x_ref[...] + (coef * gate_ref[...]) * n
    o_ref[...] = x
    hn = x * lax.rsqrt(jnp.mean(x * x, axis=-1, keepdims=True) + RMS_EPS) * g2_ref[...]
    h_ref[...] = (hn * (1.0 + sc_ref[...]) + sh_ref[...]).astype(h_ref.dtype)


def _residual(x, y, g, gate, coef, next_mod=None):
    m, d = x.shape
    tm = _pick(m, 256, 128, 8)
    row = pl.BlockSpec((1, d), lambda i: (0, 0))
    blk = pl.BlockSpec((tm, d), lambda i: (i, 0))
    if next_mod is None:
        return pl.pallas_call(
            functools.partial(_residual_kernel, coef=coef),
            grid=(m // tm,),
            in_specs=[blk, blk, row, row],
            out_specs=blk,
            out_shape=jax.ShapeDtypeStruct((m, d), F32),
            compiler_params=_params(1),
            name="residual",
        )(x, y, g.reshape(1, d), gate.reshape(1, d)), None
    return pl.pallas_call(
        functools.partial(_residual_modulate_kernel, coef=coef),
        grid=(m // tm,),
        in_specs=[blk, blk, row, row, row, row, row],
        out_specs=(blk, blk),
        out_shape=(jax.ShapeDtypeStruct((m, d), F32), jax.ShapeDtypeStruct((m, d), BF16)),
        compiler_params=_params(1),
        name="residual_modulate",
    )(x, y, g.reshape(1, d), gate.reshape(1, d), *[t.reshape(1, d) for t in next_mod])


def _mm_kernel(a_ref, w_ref, o_ref, *, sigmoid):
    y = jnp.dot(a_ref[...], w_ref[...], preferred_element_type=F32)
    o_ref[...] = (jax.nn.sigmoid(y) if sigmoid else y).astype(o_ref.dtype)


def _mm(a, w, lead, col_off, ncols, out_dtype, tm_pref=1024, tn_pref=1024, sigmoid=False, name="mm"):
    m, k = a.shape
    tm = _pick(m, *[c for c in (1024, 512, 256, 128, 8) if c <= tm_pref])
    tn = _pick(math.gcd(col_off, ncols), *[c for c in (1024, 768, 512, 384, 256, 128) if c <= tn_pref])
    off = col_off // tn
    w_spec = pl.BlockSpec((None,) * len(lead) + (k, tn), lambda i, j: tuple(lead) + (0, off + j))
    return pl.pallas_call(
        functools.partial(_mm_kernel, sigmoid=sigmoid),
        grid=(m // tm, ncols // tn),
        in_specs=[pl.BlockSpec((tm, k), lambda i, j: (i, 0)), w_spec],
        out_specs=pl.BlockSpec((tm, tn), lambda i, j: (i, j)),
        out_shape=jax.ShapeDtypeStruct((m, ncols), out_dtype),
        compiler_params=_params(2),
        name=name,
    )(a, w)


def _mm_glu_kernel(a_ref, w1_ref, w2_ref, b1_ref, b2_ref, o_ref, *, swiglu):
    a = a_ref[...].astype(BF16)
    y1 = jnp.dot(a, w1_ref[...], preferred_element_type=F32) + b1_ref[...]
    y2 = jnp.dot(a, w2_ref[...], preferred_element_type=F32) + b2_ref[...]
    out = _silu(y1) * y2 if swiglu else y1 * jax.nn.sigmoid(y2)
    o_ref[...] = out.astype(o_ref.dtype)


def _mm_glu(a, w, lead, bias, swiglu, name):
    m, k = a.shape
    half = w.shape[-1] // 2
    tm = _pick(m, *[c for c in (2048, 1024, 512, 256, 128, 8) if c <= GLU_TM])
    tn = _pick(half, 512, 256, 128)
    nj = half // tn
    nl = len(lead)
    w1 = pl.BlockSpec((None,) * nl + (k, tn), lambda i, j: tuple(lead) + (0, j))
    w2 = pl.BlockSpec((None,) * nl + (k, tn), lambda i, j: tuple(lead) + (0, nj + j))
    b1 = pl.BlockSpec((1, tn), lambda i, j: (0, j))
    b2 = pl.BlockSpec((1, tn), lambda i, j: (0, nj + j))
    return pl.pallas_call(
        functools.partial(_mm_glu_kernel, swiglu=swiglu),
        grid=(m // tm, nj),
        in_specs=[pl.BlockSpec((tm, k), lambda i, j: (i, 0)), w1, w2, b1, b2],
        out_specs=pl.BlockSpec((tm, tn), lambda i, j: (i, j)),
        out_shape=jax.ShapeDtypeStruct((m, half), BF16),
        compiler_params=_params(2),
        name=name,
    )(a, w, w, bias, bias)


def _merge_kernel(ya_ref, yb_ref, yc_ref, w_ref, ga_ref, gb_ref, gc_ref, o_ref):
    acc = ga_ref[...].astype(F32) * jnp.dot(ya_ref[...], w_ref[0], preferred_element_type=F32)
    acc += gb_ref[...].astype(F32) * jnp.dot(yb_ref[...], w_ref[1], preferred_element_type=F32)
    acc += gc_ref[...].astype(F32) * jnp.dot(yc_ref[...], w_ref[2], preferred_element_type=F32)
    o_ref[...] = acc.astype(o_ref.dtype)


def _merge(ya, yb, yc, w_branch, l, gates):
    m, bw = ya.shape
    d = w_branch.shape[-1]
    tm = _pick(m, 512, 256, 128, 8)
    tn = _pick(d, 512, 256, 128)
    nj = d // tn
    y_spec = pl.BlockSpec((tm, bw), lambda i, j: (i, 0))
    g_specs = [pl.BlockSpec((tm, tn), functools.partial(lambda i, j, k: (i, k * nj + j), k=k))
               for k in range(3)]
    return pl.pallas_call(
        _merge_kernel,
        grid=(m // tm, nj),
        in_specs=[y_spec, y_spec, y_spec,
                  pl.BlockSpec((None, 3, bw, tn), lambda i, j: (l, 0, 0, j))] + g_specs,
        out_specs=pl.BlockSpec((tm, tn), lambda i, j: (i, j)),
        out_shape=jax.ShapeDtypeStruct((m, d), BF16),
        compiler_params=_params(2),
        name="branch_merge",
    )(ya, yb, yc, w_branch, gates, gates, gates)


def _na_bias_kernel(rpb_ref, o_ref, *, kw, nr, nc):
    h = pl.program_id(0)
    w = GRID_W
    q = lax.broadcasted_iota(jnp.int32, (w, 2 * w), 0)
    c2 = lax.broadcasted_iota(jnp.int32, (w, 2 * w), 1)
    second = c2 >= w
    c = jnp.where(second, c2 - w, c2)
    dc = jnp.clip(c - q, -(kw - 1), kw - 1) + (kw - 1)
    cs = jnp.clip(q - kw // 2, 0, w - kw)
    ok = jnp.logical_and(c >= cs, c < cs + kw)
    for dr in range(nr - 1):
        t = jnp.zeros((w, 2 * w), F32)
        for dd in range(nc):
            lo = rpb_ref[(h * nr + dr) * nc + dd]
            hi = rpb_ref[(h * nr + dr + 1) * nc + dd]
            t = jnp.where(dc == dd, jnp.where(second, hi, lo), t)
        o_ref[dr] = jnp.where(ok, t, NEG_INF)


def _na_bias(rpb, kw):
    heads, nr, nc = rpb.shape
    return pl.pallas_call(
        functools.partial(_na_bias_kernel, kw=kw, nr=nr, nc=nc),
        grid=(heads,),
        in_specs=[pl.BlockSpec(memory_space=pltpu.SMEM)],
        out_specs=pl.BlockSpec((None, nr - 1, GRID_W, 2 * GRID_W), lambda h: (h, 0, 0, 0)),
        out_shape=jax.ShapeDtypeStruct((heads, nr - 1, GRID_W, 2 * GRID_W), F32),
        compiler_params=_params(1),
        name="na_bias",
    )(rpb.reshape(-1))


def _na_kernel(q_ref, k_ref, v_ref, tab_ref, o_ref, *, rb, rows, kh, kh_full, scale):
    i = pl.program_id(1)
    w = GRID_W
    scores, toks = [], []
    for rr in range(rb):
        r = i * rb + rr
        start = jnp.clip(r - kh // 2, 0, rows - kh)
        dr0 = start - r + (kh_full - 1)
        tok = pl.multiple_of(start * w, w)
        q = q_ref[rr * w:(rr + 1) * w, :]
        kwin = k_ref[pl.ds(tok, kh * w), :]
        s = lax.dot_general(q, kwin, (((1,), (1,)), ((), ())), preferred_element_type=F32) * scale
        bias = jnp.concatenate([tab_ref[dr0 + 2 * mm] for mm in range(kh // 2)], axis=1)
        scores.append(s + bias)
        toks.append(tok)
    for rr in range(rb):
        s = scores[rr]
        p = jnp.exp(s - jnp.max(s, axis=-1, keepdims=True))
        denom = jnp.sum(p, axis=-1, keepdims=True)
        o = jnp.dot(p.astype(BF16), v_ref[pl.ds(toks[rr], kh * w), :], preferred_element_type=F32)
        o_ref[rr * w:(rr + 1) * w, :] = (o / denom).astype(o_ref.dtype)


def _neighborhood_attention(qkv, rpb, kw):
    L = qkv.shape[0]
    heads, nr, _ = rpb.shape
    bw = qkv.shape[1] // 3
    hd = bw // heads
    assert hd == LANES, "neighbourhood-attention head dim must fill one lane tile"
    rows = L // GRID_W
    kh_full = (nr + 1) // 2
    kh = min(kh_full, rows)
    assert kh % 2 == 0
    tab = _na_bias(rpb, kw)
    rb = _pick(rows, 8, 4, 2, 1)
    return pl.pallas_call(
        functools.partial(_na_kernel, rb=rb, rows=rows, kh=kh, kh_full=kh_full, scale=hd ** -0.5),
        grid=(heads, rows // rb),
        in_specs=[pl.BlockSpec((rb * GRID_W, hd), lambda h, i: (i, h)),
                  pl.BlockSpec((L, hd), lambda h, i: (0, heads + h)),
                  pl.BlockSpec((L, hd), lambda h, i: (0, 2 * heads + h)),
                  pl.BlockSpec((None, nr - 1, GRID_W, 2 * GRID_W), lambda h, i: (h, 0, 0, 0))],
        out_specs=pl.BlockSpec((rb * GRID_W, hd), lambda h, i: (i, h)),
        out_shape=jax.ShapeDtypeStruct((L, bw), BF16),
        compiler_params=_params(2),
        name="neighborhood_attention",
    )(qkv, qkv, qkv, tab)


def _mm_rotary_kernel(a_ref, w_ref, cos_ref, sin_ref, o_ref, *, hd, n_q_tiles, q_scale):
    y = jnp.dot(a_ref[...], w_ref[...], preferred_element_type=F32)
    scale = jnp.where(pl.program_id(1) < n_q_tiles, q_scale, 1.0)
    cos, sin = cos_ref[...] * scale, sin_ref[...] * scale
    lane = lax.broadcasted_iota(jnp.int32, cos.shape, 1)
    first = (lane % hd) < (hd // 2)
    for t in range(y.shape[1] // LANES):
        x = y[:, t * LANES:(t + 1) * LANES]
        rot = jnp.where(first, pltpu.roll(x, LANES - hd // 2, 1), pltpu.roll(x, hd // 2, 1))
        o_ref[:, t * LANES:(t + 1) * LANES] = (x * cos + rot * sin).astype(o_ref.dtype)


def _mm_rotary(a, w, lead, col_off, ncols, hd):
    L, k = a.shape
    assert LANES % hd == 0
    half = hd // 2
    inv_freq = ROPE_THETA ** (-jnp.arange(half, dtype=F32) / half)
    ang = jnp.arange(L, dtype=F32)[:, None] * inv_freq[None, :]
    cos, sin = jnp.cos(ang), jnp.sin(ang)
    reps = LANES // hd
    cos_t = jnp.tile(jnp.concatenate([cos, cos], axis=1), (1, reps))
    sin_t = jnp.tile(jnp.concatenate([-sin, sin], axis=1), (1, reps))
    tm = _pick(L, 1024, 512, 256, 128, 8)
    tn = _pick(math.gcd(col_off, ncols // 2), 768, 512, 384, 256, 128)
    off = col_off // tn
    tab = pl.BlockSpec((tm, LANES), lambda i, j: (i, 0))
    return pl.pallas_call(
        functools.partial(_mm_rotary_kernel, hd=hd, n_q_tiles=ncols // 2 // tn,
                          q_scale=hd ** -0.5 * math.log2(math.e)),
        grid=(L // tm, ncols // tn),
        in_specs=[pl.BlockSpec((tm, k), lambda i, j: (i, 0)),
                  pl.BlockSpec((None,) * len(lead) + (k, tn), lambda i, j: tuple(lead) + (0, off + j)),
                  tab, tab],
        out_specs=pl.BlockSpec((tm, tn), lambda i, j: (i, j)),
        out_shape=jax.ShapeDtypeStruct((L, ncols), BF16),
        compiler_params=_params(2),
        name="in_proj_qk_rotary",
    )(a, w, cos_t, sin_t)


DIFF_TQ = 256
DIFF_TK = 8192
DIFF_SUB = 256
DIFF_LOOKAHEAD = 5
DIFF_ONES_ROWS = 16


def _diff_attn_kernel(q1_ref, q2_ref, k1_ref, k2_ref, v_ref, lam_ref, g_ref, o_ref,
                      qp_sc, acc1, acc2, m1, m2, *, tk, sub, hd, vd, lam_init):
    tq = q1_ref.shape[0]
    nk = v_ref.shape[0]
    par = pl.program_id(0) % (LANES // hd)
    rowi = lax.broadcasted_iota(jnp.int32, (LANES, tq), 0)
    keep = jnp.logical_and(rowi >= par * hd, rowi < (par + 1) * hd)
    for t, q_ref in enumerate((q1_ref, q2_ref)):
        qp_sc[t] = jnp.where(keep, q_ref[...].astype(F32).T, 0.0).astype(BF16)
    for acc, m in ((acc1, m1), (acc2, m2)):
        acc[...] = jnp.zeros(acc.shape, F32)
        m[...] = jnp.full(m.shape, NEG_INF, F32)

    def body(c, carry):
        off = pl.multiple_of(c * tk, tk)
        tasks = [(j, t) for j in range(tk // sub) for t in range(2)]
        k_refs, accs, ms = (k1_ref, k2_ref), (acc1, acc2), (m1, m2)

        def scores(j, t):
            ks = k_refs[t][pl.ds(off + j * sub, sub), :]
            return jnp.dot(ks, qp_sc[t], preferred_element_type=F32)

        pending = [scores(*task) for task in tasks[:DIFF_LOOKAHEAD]]
        for idx, (j, t) in enumerate(tasks):
            s = pending.pop(0)
            if idx + DIFF_LOOKAHEAD < len(tasks):
                pending.append(scores(*tasks[idx + DIFF_LOOKAHEAD]))
            m_old = ms[t][...]
            m_new = jnp.maximum(m_old, jnp.max(s, axis=0, keepdims=True))
            p = jnp.exp2(s - m_new).astype(BF16)
            pv = jnp.dot(v_ref[c, :, j * sub:(j + 1) * sub], p, preferred_element_type=F32)
            accs[t][...] = jnp.exp2(m_old - m_new) * accs[t][...] + pv
            ms[t][...] = m_new
        return carry

    lax.fori_loop(0, nk, body, 0)
    lp = lam_ref[...]
    lam = (jnp.exp(jnp.sum(lp[0:1] * lp[1:2], axis=-1, keepdims=True))
           - jnp.exp(jnp.sum(lp[2:3] * lp[3:4], axis=-1, keepdims=True)) + lam_init)
    a1, a2 = acc1[...], acc2[...]
    o = a1[:vd] / a1[vd:vd + 1] - lam * (a2[:vd] / a2[vd:vd + 1])
    o = o * lax.rsqrt(jnp.mean(o * o, axis=0, keepdims=True) + RMS_EPS) * g_ref[...]
    o_ref[...] = (o * (1.0 - lam_init)).T.astype(o_ref.dtype)


def _diff_attention(qk_rot, v, diff_lam, subln_g, lam_init):
    L = qk_rot.shape[0]
    hd = diff_lam.shape[-1]
    heads = qk_rot.shape[1] // (4 * hd)
    vd = v.shape[1] // heads
    assert vd == LANES and LANES % hd == 0 and (heads * hd) % LANES == 0
    hp = LANES // hd
    nb = heads * hd // LANES
    tq = _pick(L, DIFF_TQ, 128)
    tk = _pick(L, DIFF_TK, 4096, 2048, 1024, 512, 256, 128)
    nk = L // tk
    vt = jnp.transpose(v.reshape(nk, tk, heads, vd), (2, 0, 3, 1))
    vt = jnp.concatenate([vt, jnp.ones((heads, nk, DIFF_ONES_ROWS, tk), BF16)], axis=2)
    acc = pltpu.VMEM((vd + DIFF_ONES_ROWS, tq), F32)
    stat = pltpu.VMEM((1, tq), F32)
    return pl.pallas_call(
        functools.partial(_diff_attn_kernel, tk=tk, sub=_pick(tk, DIFF_SUB), hd=hd, vd=vd, lam_init=lam_init),
        grid=(heads, L // tq),
        in_specs=[pl.BlockSpec((tq, LANES), lambda h, i: (i, h // hp)),
                  pl.BlockSpec((tq, LANES), lambda h, i: (i, nb + h // hp)),
                  pl.BlockSpec((L, LANES), lambda h, i: (0, 2 * nb + h // hp)),
                  pl.BlockSpec((L, LANES), lambda h, i: (0, 3 * nb + h // hp)),
                  pl.BlockSpec((None, nk, vd + DIFF_ONES_ROWS, tk), lambda h, i: (h, 0, 0, 0)),
                  pl.BlockSpec(diff_lam.shape, lambda h, i: (0, 0)),
                  pl.BlockSpec((vd, 1), lambda h, i: (0, 0))],
        out_specs=pl.BlockSpec((tq, vd), lambda h, i: (i, h)),
        out_shape=jax.ShapeDtypeStruct((L, heads * vd), BF16),
        scratch_shapes=[pltpu.VMEM((2, LANES, tq), BF16), acc, acc, stat, stat],
        compiler_params=_params(2),
        name="diff_attention",
    )(qk_rot, qk_rot, qk_rot, qk_rot, vt, diff_lam, subln_g.reshape(vd, 1))


def _s5_tables(lam_re, lam_im, log_dt, b_re, b_im, c_re, c_im, d_skip):
    T = S5_CHUNK
    _, G, P, HG = b_re.shape
    dt = jnp.exp(log_dt.astype(F32))[..., None]
    lr, li = lam_re.astype(F32), lam_im.astype(F32)
    j = jnp.arange(T + 1, dtype=F32)[:, None, None, None]
    mag = jnp.exp(lr * dt * j)
    pw_re, pw_im = mag * jnp.cos(li * dt * j), mag * jnp.sin(li * dt * j)
    den = lr * lr + li * li
    nr_, ni_ = pw_re[1] - 1.0, pw_im[1]
    f_re, f_im = (nr_ * lr + ni_ * li) / den, (ni_ * lr - nr_ * li) / den
    bb_re = f_re[..., None] * b_re - f_im[..., None] * b_im
    bb_im = f_re[..., None] * b_im + f_im[..., None] * b_re
    cr, ci = c_re.astype(F32)[None], c_im.astype(F32)[None]
    cp_re = cr * pw_re[:, :, :, None, :] - ci * pw_im[:, :, :, None, :]
    cp_im = cr * pw_im[:, :, :, None, :] + ci * pw_re[:, :, :, None, :]
    hp = lax.Precision.HIGHEST
    kj = (jnp.einsum('jdghp,dgpk->jdghk', cp_re[:T], bb_re, precision=hp)
          - jnp.einsum('jdghp,dgpk->jdghk', cp_im[:T], bb_im, precision=hp))
    sig = jnp.arange(T)[:, None]
    tau = jnp.arange(T)[None, :]
    lag_f = jnp.clip(tau - sig, 0, T - 1)
    lag_b = jnp.clip(sig - tau, 0, T - 1)
    kf = jnp.where((tau >= sig)[:, :, None, None, None], kj[lag_f, 0], 0.0)
    kb = jnp.where((sig >= tau)[:, :, None, None, None], kj[lag_b, 1], 0.0)
    kin = jnp.transpose(kf + kb, (2, 0, 4, 1, 3)).reshape(G, T * HG, T * HG)
    def bend_dir(d, order):
        pr, pi = pw_re[order, d], pw_im[order, d]
        re = pr[..., None] * bb_re[d][None] - pi[..., None] * bb_im[d][None]
        im = pr[..., None] * bb_im[d][None] + pi[..., None] * bb_re[d][None]
        to_rows = lambda t: jnp.transpose(t, (1, 0, 3, 2)).reshape(G, T * HG, P)
        return to_rows(re), to_rows(im)
    bf_re, bf_im = bend_dir(0, jnp.arange(T - 1, -1, -1))
    bb_re_, bb_im_ = bend_dir(1, jnp.arange(T))
    bend = jnp.stack([bf_re, bf_im, bb_re_, bb_im_], axis=2)
    def wout_dir(d, order):
        to_cols = lambda t: jnp.transpose(t, (1, 3, 0, 2)).reshape(G, P, T * HG)
        return to_cols(cp_re[order, d]), to_cols(-cp_im[order, d])
    wf_re, wf_im = wout_dir(0, jnp.arange(1, T + 1))
    wb_re, wb_im = wout_dir(1, jnp.arange(T, 0, -1))
    wout = jnp.stack([wf_re, wf_im, wb_re, wb_im], axis=1)
    eye2 = jnp.eye(2, dtype=F32)
    G2 = G // 2
    kin2 = jnp.einsum('karc,ab->karbc', kin.reshape(G2, 2, T * HG, T * HG), eye2)
    kin2 = kin2.reshape(G2, 2 * T * HG, 2 * T * HG)
    bend2 = jnp.einsum('karqp,ab->karqbp', bend.reshape(G2, 2, T * HG, 4, P), eye2)
    bend2 = bend2.reshape(G2, 2 * T * HG, 4 * 2 * P)
    wout2 = jnp.einsum('kaqpc,ab->kqapbc', wout.reshape(G2, 2, 4, P, T * HG), eye2)
    wout2 = wout2.reshape(G2, 4 * 2 * P, 2 * T * HG)
    a = jnp.stack([pw_re[T, 0], pw_im[T, 0], pw_re[T, 1], pw_im[T, 1]], axis=1)
    a2 = jnp.transpose(a.reshape(G2, 2, 4, P), (0, 2, 1, 3)).reshape(G2, 4, 2 * P)
    dsk = jnp.tile(d_skip.astype(F32)[:, None, :], (1, T, 1)).reshape(G2, 1, 2 * T * HG)
    return kin2.astype(BF16), bend2.astype(BF16), wout2.astype(BF16), a2, dsk


def _s5_kernel(u_ref, kin_ref, bend_ref, wout_ref, a_ref, d_ref, perm_ref, o_ref, z_sc, s_sc, g_sc):
    nc = u_ref.shape[0]
    pb, cw = kin_ref.shape[0], kin_ref.shape[1]
    sw = a_ref.shape[2]
    sub = 8
    nt = nc // sub
    row = lax.broadcasted_iota(jnp.int32, (sub, sw), 0)
    for pp in range(pb):
        u = u_ref[:, pp * cw:(pp + 1) * cw]
        ub = u.astype(BF16)
        z_sc[...] = jnp.dot(ub, bend_ref[pp], preferred_element_type=F32)
        afr, afi, abr, abi = (jnp.broadcast_to(a_ref[pp, t:t + 1, :], (sub, sw)) for t in range(4))

        def body(i, carry):
            xfr, xfi, xbr, xbi = carry
            rf = pl.multiple_of(i * sub, sub)
            rb = pl.multiple_of((nt - 1 - i) * sub, sub)
            zf = z_sc[pl.ds(rf, sub), 0:2 * sw]
            zb = z_sc[pl.ds(rb, sub), 2 * sw:4 * sw]
            sfr = sfi = sbr = sbi = jnp.zeros((sub, sw), F32)
            for jf in range(sub):
                jb = sub - 1 - jf
                sfr = jnp.where(row == jf, xfr, sfr)
                sfi = jnp.where(row == jf, xfi, sfi)
                sbr = jnp.where(row == jb, xbr, sbr)
                sbi = jnp.where(row == jb, xbi, sbi)
                zfj = jnp.broadcast_to(zf[jf:jf + 1, :], (sub, 2 * sw))
                zbj = jnp.broadcast_to(zb[jb:jb + 1, :], (sub, 2 * sw))
                xfr, xfi = (afr * xfr - afi * xfi + zfj[:, 0:sw], afr * xfi + afi * xfr + zfj[:, sw:2 * sw])
                xbr, xbi = (abr * xbr - abi * xbi + zbj[:, 0:sw], abr * xbi + abi * xbr + zbj[:, sw:2 * sw])
            s_sc[pl.ds(rf, sub), 0:sw] = sfr
            s_sc[pl.ds(rf, sub), sw:2 * sw] = sfi
            s_sc[pl.ds(rb, sub), 2 * sw:3 * sw] = sbr
            s_sc[pl.ds(rb, sub), 3 * sw:4 * sw] = sbi
            return xfr, xfi, xbr, xbi

        zero = jnp.zeros((sub, sw), F32)
        lax.fori_loop(0, nt, body, (zero, zero, zero, zero))
        y = jnp.dot(ub, kin_ref[pp], preferred_element_type=F32)
        y += jnp.dot(s_sc[...].astype(BF16), wout_ref[pp], preferred_element_type=F32)
        y += d_ref[pp] * u
        g_sc[:, pp * cw:(pp + 1) * cw] = _gelu_tanh(y).astype(BF16)
    gw = cw // 2
    ngrp = 2 * pb
    for th in range(gw // LANES):
        src = jnp.concatenate([g_sc[:, g * gw + th * LANES:g * gw + (th + 1) * LANES] for g in range(ngrp)],
                              axis=1)
        res = jnp.dot(src, perm_ref[...], preferred_element_type=F32)
        o_ref[:, th * ngrp * LANES:(th + 1) * ngrp * LANES] = res.astype(o_ref.dtype)


def _s5_bidirectional(u, tables):
    kin, bend, wout, a, dsk = tables
    L, bw = u.shape
    T = S5_CHUNK
    g2 = kin.shape[0]
    cw = kin.shape[1]
    hg = cw // (2 * T)
    pb = LANES // (2 * hg)
    nc = L // T
    assert L % T == 0 and nc % 8 == 0 and g2 * 2 * hg == bw and g2 % pb == 0 and (T * hg) % LANES == 0
    up = jnp.transpose(u.reshape(nc, T, g2, 2, hg), (0, 2, 3, 1, 4)).reshape(nc, g2 * cw)
    sw4 = bend.shape[2]
    ngrp = 2 * pb
    r = jnp.arange(ngrp * LANES)
    dst = ((r % LANES) // hg) * (ngrp * hg) + (r // LANES) * hg + r % hg
    perm = jax.nn.one_hot(dst, ngrp * LANES, dtype=BF16)
    nblk = g2 // pb
    z = pl.pallas_call(
        _s5_kernel,
        grid=(nblk,),
        in_specs=[pl.BlockSpec((nc, pb * cw), lambda g: (0, g)),
                  pl.BlockSpec((pb, cw, cw), lambda g: (g, 0, 0)),
                  pl.BlockSpec((pb, cw, sw4), lambda g: (g, 0, 0)),
                  pl.BlockSpec((pb, sw4, cw), lambda g: (g, 0, 0)),
                  pl.BlockSpec((pb, 4, sw4 // 4), lambda g: (g, 0, 0)),
                  pl.BlockSpec((pb, 1, cw), lambda g: (g, 0, 0)),
                  pl.BlockSpec((ngrp * LANES, ngrp * LANES), lambda g: (0, 0))],
        out_specs=pl.BlockSpec((nc, pb * cw), lambda g: (0, g)),
        out_shape=jax.ShapeDtypeStruct((nc, g2 * cw), BF16),
        scratch_shapes=[pltpu.VMEM((nc, sw4), F32), pltpu.VMEM((nc, sw4), F32),
                        pltpu.VMEM((nc, pb * cw), BF16)],
        compiler_params=_params(1),
        name="s5_scan",
    )(up, kin, bend, wout, a, dsk, perm)
    return jnp.transpose(z.reshape(nc, nblk, T, LANES), (0, 2, 1, 3)).reshape(L, bw)


def _encoder_trunk(x, mod, wts, prm):
    L, d = x.shape
    depth = wts["ffn_w_in"].shape[0]
    bw = wts["w_branch"].shape[2]
    hd = prm["diff_lam"].shape[-1]
    kw = (prm["na_rpb"].shape[-1] + 1) // 2
    qk_cols = wts["w_in"].shape[-1] - 5 * bw - 3 * d
    zero_bias = jnp.zeros((1, wts["ffn_w_in"].shape[-1]), F32)

    def mod_rows(l, sub):
        return tuple(mod[l, (3 * sub + t) * d:(3 * sub + t + 1) * d] for t in range(3))

    def next_mod(l, sub):
        l, sub = (l, sub + 1) if sub < 2 else (l + 1, 0)
        if l == depth:
            return None
        shift, scale, _ = mod_rows(l, sub)
        return prm["norm_pre"][l, sub], scale, shift

    def ffn(x, h, l, which, sub):
        gate = mod_rows(l, sub)[2]
        act = _mm_glu(h, wts["ffn_w_in"], (l, which), zero_bias, True, "ffn_up")
        y = _mm(act, wts["ffn_w_out"], (l, which), 0, d, F32, tm_pref=FFN_DOWN_TM, tn_pref=FFN_DOWN_TN,
                name="ffn_down")
        return _residual(x, y, prm["norm_post"][l, sub], gate, 0.5, next_mod(l, sub))

    shift, scale, _ = mod_rows(0, 0)
    h = _modulate(x, prm["norm_pre"][0, 0], scale, shift)
    for l in range(depth):
        lam_init = 0.8 - 0.6 * math.exp(-0.3 * l)
        x, h = ffn(x, h, l, 0, 0)
        gate = mod_rows(l, 1)[2]
        w_in = wts["w_in"]
        qkv_a = _mm(h, w_in, (l,), 0, 3 * bw, BF16, name="in_proj_na")
        u = _mm(h, w_in, (l,), 3 * bw, bw, F32, name="in_proj_s5")
        qk_rot = _mm_rotary(h, w_in, (l,), 4 * bw, qk_cols, hd)
        vc = _mm(h, w_in, (l,), 4 * bw + qk_cols, bw, BF16, name="in_proj_v")
        gates = _mm(h, w_in, (l,), 5 * bw + qk_cols, 3 * d, BF16, sigmoid=True, name="in_proj_gates")
        y_a = _neighborhood_attention(qkv_a, prm["na_rpb"][l], kw)
        z = _s5_bidirectional(u, prm["s5_tables"][l])
        y_b = _mm_glu(z, wts["s5_w_glu"], (l,), prm["s5_b_glu"][l].reshape(1, -1), False, "s5_glu")
        y_c = _diff_attention(qk_rot, vc, prm["diff_lam"][l], prm["diff_subln"][l], lam_init)
        merged = _merge(y_a, y_b, y_c, wts["w_branch"], l, gates)
        m = _mm(merged, wts["w_out"], (l,), 0, d, F32, name="out_proj")
        x, h = _residual(x, m, prm["norm_post"][l, 1], gate, 1.0, next_mod(l, 1))
        x, h = ffn(x, h, l, 1, 2)
    return x


def kernel(x_prompt, x_sample, c_prompt, c_sample, ada_w, ada_b, norm_pre, norm_post, ffn_w_in, ffn_w_out,
           w_in, w_branch, w_out, na_rpb, s5_lam_re, s5_lam_im, s5_log_dt, s5_b_re, s5_b_im, s5_c_re,
           s5_c_im, s5_d, s5_w_glu, s5_b_glu, diff_lam, diff_subln):
    depth = ada_w.shape[0]
    wts = {"ffn_w_in": ffn_w_in.astype(BF16), "ffn_w_out": ffn_w_out.astype(BF16),
           "w_in": w_in.astype(BF16), "w_branch": w_branch.astype(BF16),
           "w_out": w_out.astype(BF16), "s5_w_glu": s5_w_glu.astype(BF16)}
    prm = {"norm_pre": norm_pre, "norm_post": norm_post, "na_rpb": na_rpb, "s5_b_glu": s5_b_glu,
           "diff_lam": diff_lam, "diff_subln": diff_subln,
           "s5_tables": [_s5_tables(s5_lam_re[l], s5_lam_im[l], s5_log_dt[l], s5_b_re[l], s5_b_im[l],
                                    s5_c_re[l], s5_c_im[l], s5_d[l]) for l in range(depth)]}
    c_all = jnp.concatenate([c_prompt, c_sample], axis=0)
    mod = _ada_mod(c_all, ada_w, ada_b)
    outs = []
    row = 0
    for x in (x_prompt, x_sample):
        ys = []
        for b in range(x.shape[0]):
            ys.append(_encoder_trunk(x[b], mod[:, row], wts, prm))
            row += 1
        outs.append(ys[0][None] if len(ys) == 1 else jnp.stack(ys, axis=0))
    return tuple(outs)
```
